```python
import math
import jax, jax.numpy as jnp
from jax import lax
import numpy as np

D_MODEL = 2048
BATCH = 32
SEQ = 256
DEPTH = 4
DEC_BATCH = 2
DEC_SEQ = 4096
PAST_LEN = 512

GRID_W = 64
HEAD_DIM = 128
ATT_HEADS = 8
ATT_KV_HEADS = 2
ATT_WIDTH = ATT_HEADS * HEAD_DIM
KV_WIDTH = ATT_KV_HEADS * HEAD_DIM
CONV_WIDTH = D_MODEL // 4
GLA_HEADS = 4
GLA_DK = 64
GLA_DV = 128
GLA_KW = GLA_HEADS * GLA_DK
GLA_VW = GLA_HEADS * GLA_DV
GLA_RANK = 16
GLA_TAU = 16.0
GLA_CHUNK = 64
Q_BLOCK = 128
ROPE_THETA = 10000.0
EPS = 1e-6
MIX_WIDTH = ATT_WIDTH + CONV_WIDTH + GLA_VW
PROJ_SIZES = (ATT_WIDTH, KV_WIDTH, KV_WIDTH, ATT_WIDTH,
              CONV_WIDTH, CONV_WIDTH, CONV_WIDTH, CONV_WIDTH,
              GLA_KW, GLA_KW, GLA_VW, GLA_VW, 2 * GLA_RANK)
SPLIT_POINTS = tuple(int(s) for s in np.cumsum(PROJ_SIZES)[:-1])
IN_WIDTH = sum(PROJ_SIZES)

kernel_name = "hybrid_diffusion_prefix_trunk_step"


def rmsnorm(x, w):
    xf = x.astype(jnp.float32)
    y = xf * lax.rsqrt(jnp.mean(xf * xf, axis=-1, keepdims=True) + EPS)
    return (y * w.astype(jnp.float32)).astype(x.dtype)


def rope_2d(x):
    t = x.shape[1]
    pos = jnp.arange(t)
    row = (pos // GRID_W).astype(jnp.float32)
    col = (pos % GRID_W).astype(jnp.float32)
    n_freq = HEAD_DIM // 4
    inv = ROPE_THETA ** (-jnp.arange(n_freq, dtype=jnp.float32) / n_freq)

    def rot(xh, p):
        ang = p[:, None] * inv[None, :]
        cos = jnp.cos(ang)[None, :, None, :]
        sin = jnp.sin(ang)[None, :, None, :]
        x1, x2 = jnp.split(xh, 2, axis=-1)
        return jnp.concatenate([x1 * cos - x2 * sin, x1 * sin + x2 * cos], axis=-1)

    xf = x.astype(jnp.float32)
    half = HEAD_DIM // 2
    return jnp.concatenate([rot(xf[..., :half], row), rot(xf[..., half:], col)], axis=-1).astype(x.dtype)


def block_attention(q, k, v):
    b, tq, h, hd = q.shape
    kvh = k.shape[2]
    g = h // kvh
    nb = tq // Q_BLOCK
    qb = q.reshape(b, nb, Q_BLOCK, kvh, g, hd).transpose(1, 0, 2, 3, 4, 5)
    kf = k.astype(jnp.float32)
    vf = v.astype(jnp.float32)
    scale = hd ** -0.5

    def one(qblk):
        s = jnp.einsum('bqkgd,bskd->bkgqs', qblk.astype(jnp.float32), kf) * scale
        p = jax.nn.softmax(s, axis=-1)
        return jnp.einsum('bkgqs,bskd->bqkgd', p, vf).astype(q.dtype)

    o = lax.map(one, qb)
    return o.transpose(1, 0, 2, 3, 4, 5).reshape(b, tq, h * hd)


def conv3_centred(u, w):
    up = jnp.pad(u, ((0, 0), (1, 1), (0, 0)))
    return up[:, :-2] * w[0] + up[:, 1:-1] * w[1] + up[:, 2:] * w[2]


def gla_scan(q, k, v, logf, s0):
    b, t, h, dk = q.shape
    dv = v.shape[-1]
    n = t // GLA_CHUNK

    def chunks(a):
        return a.reshape(b, n, GLA_CHUNK, h, a.shape[-1]).transpose(1, 0, 3, 2, 4).astype(jnp.float32)

    qc, kc, vc, gc = chunks(q), chunks(k), chunks(v), chunks(logf)
    causal = jnp.tril(jnp.ones((GLA_CHUNK, GLA_CHUNK), dtype=bool))[None, None, :, :, None]

    def step(S, inp):
        qi, ki, vi, gi = inp
        bc = jnp.cumsum(gi, axis=2)
        inter = jnp.einsum('bhld,bhde->bhle', qi * jnp.exp(bc), S)
        diff = bc[:, :, :, None, :] - bc[:, :, None, :, :]
        dec = jnp.exp(jnp.where(causal, diff, -jnp.inf))
        att = jnp.einsum('bhtd,bhtsd,bhsd->bhts', qi, dec, ki)
        intra = jnp.einsum('bhts,bhse->bhte', att, vi)
        blast = bc[:, :, -1:, :]
        S = jnp.exp(blast[:, :, 0, :])[..., None] * S + jnp.einsum('bhsd,bhse->bhde', ki * jnp.exp(blast - bc), vi)
        return S, inter + intra

    S, o = lax.scan(step, s0.astype(jnp.float32), (qc, kc, vc, gc))
    o = o.transpose(1, 0, 3, 2, 4).reshape(b, t, h, dv)
    return o, S


def gla_bidir(q, k, v, logf_f, logf_b, s0_f, s0_b):
    flip = lambda a: jnp.flip(a, axis=1)
    of, sf = gla_scan(q, k, v, logf_f, s0_f)
    ob, sb = gla_scan(flip(q), flip(k), flip(v), flip(logf_b), s0_b)
    return of + flip(ob), sf, sb


def mixer_layer(x, mod, norm_w, w_in, q_norm_w, k_norm_w, conv_w, a_up, a_bias, gla_norm_w, w_out, ctx):
    b, t, _ = x.shape
    shift, scale, gate = jnp.split(mod, 3, axis=-1)
    xn = rmsnorm(x, norm_w) * (1.0 + scale[:, None, :]) + shift[:, None, :]
    proj = xn @ w_in
    aq, ak, av, ag, ch, cb, cc, cg, gq, gk, gv, gg, glr = jnp.split(proj, SPLIT_POINTS, axis=-1)

    q = rmsnorm(aq.reshape(b, t, ATT_HEADS, HEAD_DIM), q_norm_w)
    k = rmsnorm(ak.reshape(b, t, ATT_KV_HEADS, HEAD_DIM), k_norm_w)
    v = av.reshape(b, t, ATT_KV_HEADS, HEAD_DIM)
    if ctx is None:
        k_all, v_all = k, v
    else:
        k_ctx, v_ctx, s0_f, s0_b = ctx
        q = rope_2d(q)
        k = rope_2d(k)
        k_all = jnp.concatenate([k, k_ctx.astype(k.dtype)], axis=1)
        v_all = jnp.concatenate([v, v_ctx.astype(v.dtype)], axis=1)
    out_a = block_attention(q, k_all, v_all) * jax.nn.silu(ag)

    out_b = cb * conv3_centred(cc * ch, conv_w) * jax.nn.silu(cg)

    lr = glr.reshape(b, t, 2, GLA_RANK)
    logits = jnp.einsum('btzr,zrk->btzk', lr, a_up) + a_bias
    logf = jax.nn.log_sigmoid(logits.astype(jnp.float32)) / GLA_TAU
    logf_f = logf[:, :, 0].reshape(b, t, GLA_HEADS, GLA_DK)
    logf_b = logf[:, :, 1].reshape(b, t, GLA_HEADS, GLA_DK)
    gq_h = gq.reshape(b, t, GLA_HEADS, GLA_DK) * (GLA_DK ** -0.5)
    gk_h = gk.reshape(b, t, GLA_HEADS, GLA_DK)
    gv_h = gv.reshape(b, t, GLA_HEADS, GLA_DV)
    if ctx is None:
        s0_f = jnp.zeros((b, GLA_HEADS, GLA_DK, GLA_DV), jnp.float32)
        s0_b = s0_f
    o_c, s_f, s_b = gla_bidir(gq_h, gk_h, gv_h, logf_f, logf_b, s0_f, s0_b)
    out_c = rmsnorm(o_c, gla_norm_w).astype(x.dtype).reshape(b, t, GLA_VW) * jax.nn.silu(gg)

    mix = jnp.concatenate([out_a, out_b, out_c], axis=-1) @ w_out
    x = x + gate[:, None, :] * mix
    return x, k, v, s_f, s_b


def setup_inputs(seed: int = 0) -> dict:
    key = jax.random.key(seed)
    ks = jax.random.split(key, 20)
    f32 = jnp.float32
    nrm = lambda k, s: jax.random.normal(k, s, f32)
    return {
        "x_prompt": nrm(ks[0], (BATCH, SEQ, D_MODEL)),
        "x_sample": nrm(ks[1], (DEC_BATCH, DEC_SEQ, D_MODEL)),
        "cache_k": nrm(ks[2], (DEC_BATCH, DEPTH, PAST_LEN, ATT_KV_HEADS, HEAD_DIM)),
        "cache_v": nrm(ks[3], (DEC_BATCH, DEPTH, PAST_LEN, ATT_KV_HEADS, HEAD_DIM)),
        "state_gla": 0.5 * nrm(ks[4], (DEC_BATCH, DEPTH, 2, GLA_HEADS, GLA_DK, GLA_DV)),
        "c": nrm(ks[5], (DEC_BATCH, D_MODEL)),
        "c_ctx": nrm(ks[6], (D_MODEL,)),
        "w_mod": 0.5 * nrm(ks[7], (DEPTH, D_MODEL, 3 * D_MODEL)) * D_MODEL ** -0.5,
        "b_mod": 0.01 * nrm(ks[8], (DEPTH, 3 * D_MODEL)),
        "norm_w": 1.0 + 0.02 * nrm(ks[9], (DEPTH, D_MODEL)),
        "w_in": nrm(ks[10], (DEPTH, D_MODEL, IN_WIDTH)) * D_MODEL ** -0.5,
        "q_norm_w": 1.0 + 0.02 * nrm(ks[11], (DEPTH, HEAD_DIM)),
        "k_norm_w": 1.0 + 0.02 * nrm(ks[12], (DEPTH, HEAD_DIM)),
        "conv_w": nrm(ks[13], (DEPTH, 3, CONV_WIDTH)) * 3 ** -0.5,
        "gla_a_up": nrm(ks[14], (DEPTH, 2, GLA_RANK, GLA_KW)) * GLA_RANK ** -0.5,
        "gla_a_bias": 0.1 * nrm(ks[15], (DEPTH, 2, GLA_KW)),
        "gla_norm_w": 1.0 + 0.02 * nrm(ks[16], (DEPTH, GLA_DV)),
        "w_out": nrm(ks[17], (DEPTH, MIX_WIDTH, D_MODEL)) * MIX_WIDTH ** -0.5,
        "final_norm_w": 1.0 + 0.02 * nrm(ks[18], (D_MODEL,)),
    }


def reference(x_prompt, x_sample, cache_k, cache_v, state_gla, c, c_ctx, w_mod, b_mod, norm_w, w_in,
              q_norm_w, k_norm_w, conv_w, gla_a_up, gla_a_bias, gla_norm_w, w_out, final_norm_w):
    h = x_prompt
    ks_out, vs_out, st_out = [], [], []
    for l in range(DEPTH):
        mod = jax.nn.silu(c_ctx)[None, :] @ w_mod[l] + b_mod[l]
        h, k_l, v_l, sf, sb = mixer_layer(h, mod, norm_w[l], w_in[l], q_norm_w[l], k_norm_w[l], conv_w[l],
                                          gla_a_up[l], gla_a_bias[l], gla_norm_w[l], w_out[l], None)
        ks_out.append(k_l)
        vs_out.append(v_l)
        st_out.append(jnp.stack([sf, sb], axis=1))
    y_prompt = rmsnorm(h, final_norm_w)
    new_k = jnp.stack(ks_out, axis=1)
    new_v = jnp.stack(vs_out, axis=1)
    new_state_gla = jnp.stack(st_out, axis=1).astype(x_prompt.dtype)

    z = x_sample
    for l in range(DEPTH):
        mod = jax.nn.silu(c) @ w_mod[l] + b_mod[l]
        ctx = (cache_k[:, l], cache_v[:, l], state_gla[:, l, 0], state_gla[:, l, 1])
        z, _, _, _, _ = mixer_layer(z, mod, norm_w[l], w_in[l], q_norm_w[l], k_norm_w[l], conv_w[l],
                                    gla_a_up[l], gla_a_bias[l], gla_norm_w[l], w_out[l], ctx)
    y_sample = rmsnorm(z, final_norm_w)
    return (y_prompt, y_sample, new_k, new_v, new_state_gla)
```

```python
import functools

import jax
import jax.numpy as jnp
from jax import lax
from jax.experimental import pallas as pl
from jax.experimental.pallas import tpu as pltpu

F32 = jnp.float32
BF16 = jnp.bfloat16

GRID_W = 64
HEAD_DIM = 128
ATT_HEADS = 8
ATT_KV_HEADS = 2
HEADS_PER_KV = ATT_HEADS // ATT_KV_HEADS
ATT_WIDTH = ATT_HEADS * HEAD_DIM
KV_WIDTH = ATT_KV_HEADS * HEAD_DIM
GLA_HEADS = 4
GLA_DK = 64
GLA_DV = 128
GLA_KW = GLA_HEADS * GLA_DK
GLA_VW = GLA_HEADS * GLA_DV
GLA_RANK = 16
GLA_TAU = 16.0
ROPE_THETA = 10000.0
EPS = 1e-6

COL_Q = 0
COL_K = 1024
COL_V = 1280
COL_AG = 1536
COL_CH = 2560
COL_CB = 3072
COL_CC = 3584
COL_CG = 4096
COL_GQK = 4608
COL_GV = 5120
COL_GG = 5632
COL_LR = 6144
IN_WIDTH = 6176
IN_TILE = 512
IN_WIDTH_PADDED = 6656
LR_BLOCK = 128

GLA_CHUNK = 64
GLA_SUB = 16
GLA_BLOCK = 256
GLA_EXP_CLAMP = 60.0

VMEM_LIMIT = 56 * 1024 * 1024


def _silu(x):
    return x * jax.nn.sigmoid(x)


def _rms(x, w):
    return x * lax.rsqrt(jnp.mean(x * x, axis=-1, keepdims=True) + EPS) * w


def _rope(y, cos, sin_signed):
    lane = lax.broadcasted_iota(jnp.int32, y.shape, 1)
    partner = jnp.where((lane & 63) < 32, pltpu.roll(y, 96, axis=1), pltpu.roll(y, 32, axis=1))
    return y * cos + partner * sin_signed


def _mod_kernel(c_ref, w_ref, b_ref, o_ref):
    a = _silu(c_ref[...]).astype(BF16)
    o_ref[0] = jnp.dot(a, w_ref[0].astype(BF16), preferred_element_type=F32) + b_ref[0]


def _modulation(cond, w_mod, b_mod):
    depth, d, n = w_mod.shape
    tn = 768
    return pl.pallas_call(
        _mod_kernel,
        grid=(depth, n // tn),
        in_specs=[pl.BlockSpec((8, d), lambda l, j: (0, 0)),
                  pl.BlockSpec((1, d, tn), lambda l, j: (l, 0, j)),
                  pl.BlockSpec((1, 1, tn), lambda l, j: (l, 0, j))],
        out_specs=pl.BlockSpec((1, 8, tn), lambda l, j: (l, 0, j)),
        out_shape=jax.ShapeDtypeStruct((depth, 8, n), F32),
        compiler_params=pltpu.CompilerParams(dimension_semantics=("parallel", "parallel"),
                                             vmem_limit_bytes=VMEM_LIMIT),
        name="modulation",
    )(cond, w_mod, b_mod.reshape(depth, 1, n))


def _in_kernel(x_ref, nw_ref, shift_ref, scale_ref, w_ref, o_ref, xn_ref):
    @pl.when(pl.program_id(1) == 0)
    def _():
        xn = _rms(x_ref[...], nw_ref[...]) * (1.0 + scale_ref[0]) + shift_ref[0]
        xn_ref[...] = xn.astype(BF16)

    o_ref[...] = jnp.dot(xn_ref[...], w_ref[...], preferred_element_type=F32).astype(BF16)


def _in_projection(x, mod3, norm_w, w_in, mod_row):
    t, d = x.shape
    tm = 1024
    return pl.pallas_call(
        _in_kernel,
        grid=(t // tm, IN_WIDTH_PADDED // IN_TILE),
        in_specs=[pl.BlockSpec((tm, d), lambda i, j: (i, 0)),
                  pl.BlockSpec((1, d), lambda i, j: (0, 0)),
                  pl.BlockSpec((1, 1, d), lambda i, j: (mod_row(i * tm), 0, 0)),
                  pl.BlockSpec((1, 1, d), lambda i, j: (mod_row(i * tm), 0, 1)),
                  pl.BlockSpec((d, IN_TILE), lambda i, j: (0, j))],
        out_specs=pl.BlockSpec((tm, IN_TILE), lambda i, j: (i, j)),
        out_shape=jax.ShapeDtypeStruct((t, IN_WIDTH_PADDED), BF16),
        scratch_shapes=[pltpu.VMEM((tm, d), BF16)],
        compiler_params=pltpu.CompilerParams(dimension_semantics=("parallel", "arbitrary"),
                                             vmem_limit_bytes=VMEM_LIMIT),
        name="in_projection",
    )(x, norm_w.reshape(1, d), mod3, mod3, w_in)


def _kn_kernel(*refs, rope, emit_f32):
    k_ref, kw_ref = refs[:2]
    pos = 2
    if rope:
        cos_ref, sin_ref = refs[2:4]
        pos = 4
    o_ref = refs[pos]
    for g in range(ATT_KV_HEADS):
        cols = slice(g * HEAD_DIM, (g + 1) * HEAD_DIM)
        y = _rms(k_ref[:, cols].astype(F32), kw_ref[...])
        if rope:
            y = _rope(y, cos_ref[...], sin_ref[...])
        o_ref[:, cols] = y.astype(BF16)
        if emit_f32:
            refs[pos + 1][:, cols] = y


def _key_norm(proj, k_norm_w, rope_tables, seq_len, emit_f32):
    t = proj.shape[0]
    tr = min(1024, seq_len)
    rope = rope_tables is not None
    in_specs = [pl.BlockSpec((tr, KV_WIDTH), lambda i: (i, COL_K // KV_WIDTH)),
                pl.BlockSpec((1, HEAD_DIM), lambda i: (0, 0))]
    args = [proj, k_norm_w.reshape(1, HEAD_DIM)]
    if rope:
        per_seq = seq_len // tr
        in_specs += [pl.BlockSpec((tr, HEAD_DIM), lambda i: (i % per_seq, 0))] * 2
        args += list(rope_tables)
    out_specs = [pl.BlockSpec((tr, KV_WIDTH), lambda i: (i, 0))]
    out_shape = [jax.ShapeDtypeStruct((t, KV_WIDTH), BF16)]
    if emit_f32:
        out_specs.append(pl.BlockSpec((tr, KV_WIDTH), lambda i: (i, 0)))
        out_shape.append(jax.ShapeDtypeStruct((t, KV_WIDTH), F32))
    return pl.pallas_call(
        functools.partial(_kn_kernel, rope=rope, emit_f32=emit_f32),
        grid=(t // tr,),
        in_specs=in_specs,
        out_specs=out_specs,
        out_shape=out_shape,
        compiler_params=pltpu.CompilerParams(dimension_semantics=("parallel",)),
        name="key_norm",
    )(*args)


def _attn_kernel(*refs, tq, tk, n_self, has_ctx, rope):
    q_ref, gate_ref, k_ref, v_ref = refs[:4]
    pos = 4
    if has_ctx:
        kc_ref, vc_ref = refs[pos:pos + 2]
        pos += 2
    qw_ref = refs[pos]
    pos += 1
    if rope:
        cos_ref, sin_ref = refs[pos:pos + 2]
        pos += 2
    o_ref, qt_ref, m_ref, l_ref, acc_ref = refs[pos:pos + 5]

    q = q_ref[...].astype(F32)
    for h in range(HEADS_PER_KV):
        y = _rms(q[:, h * HEAD_DIM:(h + 1) * HEAD_DIM], qw_ref[...])
        if rope:
            y = _rope(y, cos_ref[...], sin_ref[...])
        qt_ref[:, h * tq:(h + 1) * tq] = (y * HEAD_DIM ** -0.5).T.astype(BF16)
    m_ref[...] = jnp.full(m_ref.shape, -jnp.inf, F32)
    l_ref[...] = jnp.zeros(l_ref.shape, F32)
    acc_ref[...] = jnp.zeros(acc_ref.shape, F32)

    def step(kc, vc):
        s = jnp.dot(kc, qt_ref[...], preferred_element_type=F32)
        m_old = m_ref[...]
        m_new = jnp.maximum(m_old, jnp.max(s, axis=0, keepdims=True))
        p = jnp.exp(s - m_new)
        alpha = jnp.exp(m_old - m_new)
        l_ref[...] = alpha * l_ref[...] + jnp.sum(p, axis=0, keepdims=True)
        pv = lax.dot_general(vc, p.astype(BF16), (((0,), (0,)), ((), ())),
                             preferred_element_type=F32)
        acc_ref[...] = alpha * acc_ref[...] + pv
        m_ref[...] = m_new

    def body(c, carry):
        off = pl.multiple_of(c * tk, tk)
        step(k_ref[pl.ds(off, tk), :], v_ref[pl.ds(off, tk), :])
        return carry

    lax.fori_loop(0, n_self, body, 0)
    if has_ctx:
        step(kc_ref[...], vc_ref[...])

    o = acc_ref[...] / l_ref[...]
    gate = gate_ref[...].astype(F32)
    for h in range(HEADS_PER_KV):
        cols = slice(h * HEAD_DIM, (h + 1) * HEAD_DIM)
        o_ref[:, cols] = (o[:, h * tq:(h + 1) * tq].T * _silu(gate[:, cols])).astype(BF16)


def _attention(proj, kn, q_norm_w, ctx_kv, rope_tables, batch, seq_len):
    t = proj.shape[0]
    tq = 256
    tk = min(512, seq_len)
    nq = seq_len // tq
    gw = HEADS_PER_KV * HEAD_DIM
    has_ctx = ctx_kv is not None
    rope = rope_tables is not None
    in_specs = [pl.BlockSpec((tq, gw), lambda b, g, i: (b * nq + i, COL_Q // gw + g)),
                pl.BlockSpec((tq, gw), lambda b, g, i: (b * nq + i, COL_AG // gw + g)),
                pl.BlockSpec((seq_len, HEAD_DIM), lambda b, g, i: (b, g)),
                pl.BlockSpec((seq_len, HEAD_DIM), lambda b, g, i: (b, COL_V // HEAD_DIM + g))]
    args = [proj, proj, kn, proj]
    if has_ctx:
        past = ctx_kv[0].shape[0] // batch
        in_specs += [pl.BlockSpec((past, HEAD_DIM), lambda b, g, i: (b, g))] * 2
        args += list(ctx_kv)
    in_specs.append(pl.BlockSpec((1, HEAD_DIM), lambda b, g, i: (0, 0)))
    args.append(q_norm_w.reshape(1, HEAD_DIM))
    if rope:
        in_specs += [pl.BlockSpec((tq, HEAD_DIM), lambda b, g, i: (i, 0))] * 2
        args += list(rope_tables)
    return pl.pallas_call(
        functools.partial(_attn_kernel, tq=tq, tk=tk, n_self=seq_len // tk, has_ctx=has_ctx, rope=rope),
        grid=(batch, ATT_KV_HEADS, nq),
        in_specs=in_specs,
        out_specs=pl.BlockSpec((tq, gw), lambda b, g, i: (b * nq + i, g)),
        out_shape=jax.ShapeDtypeStruct((t, ATT_WIDTH), BF16),
        scratch_shapes=[pltpu.VMEM((HEAD_DIM, HEADS_PER_KV * tq), BF16),
                        pltpu.VMEM((1, HEADS_PER_KV * tq), F32),
                        pltpu.VMEM((1, HEADS_PER_KV * tq), F32),
                        pltpu.VMEM((HEAD_DIM, HEADS_PER_KV * tq), F32)],
        compiler_params=pltpu.CompilerParams(dimension_semantics=("parallel", "parallel", "arbitrary"),
                                             vmem_limit_bytes=VMEM_LIMIT),
        name="attention",
    )(*args)


def _log_sigmoid(x):
    return -(jnp.maximum(-x, 0.0) + jnp.log1p(jnp.exp(-jnp.abs(x))))


def _gla_chunk(qk, v, lr, a, bias, st_ref, o_ref, rows, rev):
    c, s = GLA_CHUNK, GLA_SUB
    q = qk[:, :GLA_KW].astype(F32) * GLA_DK ** -0.5
    k = qk[:, GLA_KW:].astype(F32)
    logits = jnp.dot(lr, a, preferred_element_type=F32) + bias
    g = _log_sigmoid(logits) * (1.0 / GLA_TAU)

    g1 = g.astype(BF16)
    r1 = g - g1.astype(F32)
    g2 = r1.astype(BF16)
    g3 = (r1 - g2.astype(F32)).astype(BF16)
    ti = lax.broadcasted_iota(jnp.int32, (c, c), 0)
    si = lax.broadcasted_iota(jnp.int32, (c, c), 1)
    causal = (ti <= si) if rev else (ti >= si)
    tri = jnp.where(causal, 1.0, 0.0).astype(BF16)
    b = (jnp.dot(tri, g1, preferred_element_type=F32) + jnp.dot(tri, g2, preferred_element_type=F32)
         + jnp.dot(tri, g3, preferred_element_type=F32))

    b_total = b[0:1] if rev else b[c - 1:c]
    q_inter = (q * jnp.exp(b)).astype(BF16)
    k_state = (k * jnp.exp(b_total - b)).astype(BF16)
    decay_total = jnp.exp(b_total)

    srow = lax.broadcasted_iota(jnp.int32, (c, GLA_KW), 0)
    pairs = []
    for i in range(c // s):
        lo, hi = i * s, (i + 1) * s
        if rev:
            ref = b[hi:hi + 1] if hi < c else jnp.zeros((1, GLA_KW), F32)
            valid = srow >= lo
        else:
            ref = b[lo - 1:lo] if lo > 0 else jnp.zeros((1, GLA_KW), F32)
            valid = srow < hi
        q_hat = (q[lo:hi] * jnp.exp(b[lo:hi] - ref)).astype(BF16)
        expo = jnp.where(valid, jnp.minimum(ref - b, GLA_EXP_CLAMP), -jnp.inf)
        k_hat = (k * jnp.exp(expo)).astype(BF16)
        pairs.append((q_hat, k_hat))

    for h in range(GLA_HEADS):
        kc = slice(h * GLA_DK, (h + 1) * GLA_DK)
        vc = slice(h * GLA_DV, (h + 1) * GLA_DV)
        att = jnp.concatenate(
            [lax.dot_general(qh[:, kc], kh[:, kc], (((1,), (1,)), ((), ())), preferred_element_type=F32)
             for qh, kh in pairs], axis=0)
        att = jnp.where(causal, att, 0.0).astype(BF16)
        vh = v[:, vc]
        st = st_ref[h]
        o = jnp.dot(att, vh, preferred_element_type=F32) + lax.dot_general(
            q_inter[:, kc], st.astype(BF16), (((1,), (1,)), ((), ())), preferred_element_type=F32)
        o_ref[rows, vc] = o.astype(BF16)
        st_ref[h] = st * decay_total[:, kc] + lax.dot_general(
            vh, k_state[:, kc], (((0,), (0,)), ((), ())), preferred_element_type=F32)


def _gla_kernel(qkf_ref, vf_ref, lrf_ref, qkb_ref, vb_ref, lrb_ref, a_ref, bias_ref, s0_ref,
                of_ref, ob_ref, so_ref, st_ref):
    j = pl.program_id(1)

    @pl.when(j == 0)
    def _():
        for z in range(2):
            for h in range(GLA_HEADS):
                st_ref[z, h] = s0_ref[0, z, h].T

    n_chunks = GLA_BLOCK // GLA_CHUNK
    for c in range(n_chunks):
        rows = slice(c * GLA_CHUNK, (c + 1) * GLA_CHUNK)
        _gla_chunk(qkf_ref[rows, :], vf_ref[rows, :], lrf_ref[rows, :], a_ref[0], bias_ref[0],
                   st_ref.at[0], of_ref, rows, rev=False)
        rows = slice((n_chunks - 1 - c) * GLA_CHUNK, (n_chunks - c) * GLA_CHUNK)
        _gla_chunk(qkb_ref[rows, :], vb_ref[rows, :], lrb_ref[rows, :], a_ref[1], bias_ref[1],
                   st_ref.at[1], ob_ref, rows, rev=True)

    @pl.when(j == pl.num_programs(1) - 1)
    def _():
        for z in range(2):
            for h in range(GLA_HEADS):
                so_ref[0, z, h] = st_ref[z, h].T


def _gla(proj, a_pad, bias, s0, batch, seq_len):
    t = proj.shape[0]
    nb = seq_len // GLA_BLOCK
    fwd = lambda b, j: b * nb + j
    bwd = lambda b, j: b * nb + nb - 1 - j
    qk_w = 2 * GLA_KW

    def proj_specs(row):
        return [pl.BlockSpec((GLA_BLOCK, qk_w), lambda b, j: (row(b, j), COL_GQK // qk_w)),
                pl.BlockSpec((GLA_BLOCK, GLA_VW), lambda b, j: (row(b, j), COL_GV // GLA_VW)),
                pl.BlockSpec((GLA_BLOCK, LR_BLOCK), lambda b, j: (row(b, j), COL_LR // LR_BLOCK))]

    state_spec = pl.BlockSpec((1, 2, GLA_HEADS, GLA_DK, GLA_DV), lambda b, j: (b, 0, 0, 0, 0))
    return pl.pallas_call(
        _gla_kernel,
        grid=(batch, nb),
        in_specs=proj_specs(fwd) + proj_specs(bwd) + [
            pl.BlockSpec((2, LR_BLOCK, GLA_KW), lambda b, j: (0, 0, 0)),
            pl.BlockSpec((2, 1, GLA_KW), lambda b, j: (0, 0, 0)),
            state_spec],
        out_specs=[pl.BlockSpec((GLA_BLOCK, GLA_VW), lambda b, j: (fwd(b, j), 0)),
                   pl.BlockSpec((GLA_BLOCK, GLA_VW), lambda b, j: (bwd(b, j), 0)),
                   state_spec],
        out_shape=[jax.ShapeDtypeStruct((t, GLA_VW), BF16),
                   jax.ShapeDtypeStruct((t, GLA_VW), BF16),
                   jax.ShapeDtypeStruct((batch, 2, GLA_HEADS, GLA_DK, GLA_DV), F32)],
        scratch_shapes=[pltpu.VMEM((2, GLA_HEADS, GLA_DV, GLA_DK), F32)],
        compiler_params=pltpu.CompilerParams(dimension_semantics=("parallel", "arbitrary"),
                                             vmem_limit_bytes=VMEM_LIMIT),
        name="gla",
    )(proj, proj, proj, proj, proj, proj, a_pad, bias, s0)


HALO = 16


def _out_kernel(*refs, tm, seq_len, final):
    (x_ref, a_ref, ch_ref, cb_ref, cc_ref, cg_ref, chp_ref, ccp_ref, chn_ref, ccn_ref,
     gg_ref, of_ref, ob_ref, cw_ref, gnw_ref, gate_ref, w_ref) = refs[:17]
    pos = 17
    if final:
        fnw_ref = refs[pos]
        pos += 1
    o_ref, mix_ref = refs[pos:pos + 2]

    mix_ref[:, :ATT_WIDTH] = a_ref[...]

    u = cc_ref[...].astype(F32) * ch_ref[...].astype(F32)
    u_before = ccp_ref[HALO - 1:HALO, :].astype(F32) * chp_ref[HALO - 1:HALO, :].astype(F32)
    u_after = ccn_ref[0:1, :].astype(F32) * chn_ref[0:1, :].astype(F32)
    row = lax.broadcasted_iota(jnp.int32, u.shape, 0)
    seq_pos = (pl.program_id(0) * tm + row) & (seq_len - 1)
    u_prev = jnp.where(row == 0, u_before, pltpu.roll(u, 1, axis=0))
    u_prev = jnp.where(seq_pos == 0, 0.0, u_prev)
    u_next = jnp.where(row == tm - 1, u_after, pltpu.roll(u, tm - 1, axis=0))
    u_next = jnp.where(seq_pos == seq_len - 1, 0.0, u_next)
    conv = u_prev * cw_ref[0:1, :] + u * cw_ref[1:2, :] + u_next * cw_ref[2:3, :]
    out_b = cb_ref[...].astype(F32) * conv * _silu(cg_ref[...].astype(F32))
    mix_ref[:, ATT_WIDTH:ATT_WIDTH + out_b.shape[1]] = out_b.astype(BF16)

    o = of_ref[...].astype(F32) + ob_ref[...].astype(F32)
    gg = gg_ref[...].astype(F32)
    base = mix_ref.shape[1] - GLA_VW
    for h in range(GLA_HEADS):
        cols = slice(h * GLA_DV, (h + 1) * GLA_DV)
        y = _rms(o[:, cols], gnw_ref[...]) * _silu(gg[:, cols])
        mix_ref[:, base + h * GLA_DV:base + (h + 1) * GLA_DV] = y.astype(BF16)

    r = jnp.dot(mix_ref[...], w_ref[...], preferred_element_type=F32)
    xn = x_ref[...] + gate_ref[0] * r
    if final:
        xn = _rms(xn, fnw_ref[...])
    o_ref[...] = xn


def _out_projection(x, att, proj, o_f, o_b, conv_w, gla_norm_w, mod3, w_out, final_norm_w, mod_row, seq_len):
    t, d = x.shape
    tm = 256
    cw = conv_w.shape[1]
    final = final_norm_w is not None
    per_halo = tm // HALO
    last_halo = t // HALO - 1

    def col(c, width):
        return lambda i: (i, c // width)

    prev = lambda c: (lambda i: (jnp.maximum(i * per_halo - 1, 0), c // cw))
    nxt = lambda c: (lambda i: (jnp.minimum((i + 1) * per_halo, last_halo), c // cw))
    in_specs = [pl.BlockSpec((tm, d), lambda i: (i, 0)),
                pl.BlockSpec((tm, ATT_WIDTH), lambda i: (i, 0)),
                pl.BlockSpec((tm, cw), col(COL_CH, cw)),
                pl.BlockSpec((tm, cw), col(COL_CB, cw)),
                pl.BlockSpec((tm, cw), col(COL_CC, cw)),
                pl.BlockSpec((tm, cw), col(COL_CG, cw)),
                pl.BlockSpec((HALO, cw), prev(COL_CH)),
                pl.BlockSpec((HALO, cw), prev(COL_CC)),
                pl.BlockSpec((HALO, cw), nxt(COL_CH)),
                pl.BlockSpec((HALO, cw), nxt(COL_CC)),
                pl.BlockSpec((tm, GLA_VW), col(COL_GG, GLA_VW)),
                pl.BlockSpec((tm, GLA_VW), lambda i: (i, 0)),
                pl.BlockSpec((tm, GLA_VW), lambda i: (i, 0)),
                pl.BlockSpec((3, cw), lambda i: (0, 0)),
                pl.BlockSpec((1, GLA_DV), lambda i: (0, 0)),
                pl.BlockSpec((1, 1, d), lambda i: (mod_row(i * tm), 0, 2)),
                pl.BlockSpec(w_out.shape, lambda i: (0, 0))]
    args = [x, att, proj, proj, proj, proj, proj, proj, proj, proj, proj, o_f, o_b,
            conv_w, gla_norm_w.reshape(1, GLA_DV), mod3, w_out]
    if final:
        in_specs.append(pl.BlockSpec((1, d), lambda i: (0, 0)))
        args.append(final_norm_w.reshape(1, d))
    return pl.pallas_call(
        functools.partial(_out_kernel, tm=tm, seq_len=seq_len, final=final),
        grid=(t // tm,),
        in_specs=in_specs,
        out_specs=pl.BlockSpec((tm, d), lambda i: (i, 0)),
        out_shape=jax.ShapeDtypeStruct((t, d), F32),
        scratch_shapes=[pltpu.VMEM((tm, w_out.shape[0]), BF16)],
        compiler_params=pltpu.CompilerParams(dimension_semantics=("parallel",),
                                             vmem_limit_bytes=VMEM_LIMIT),
        name="out_projection",
    )(*args)


def _rope_tables(seq_len):
    pos = jnp.arange(seq_len)
    row = (pos // GRID_W).astype(F32)
    col = (pos % GRID_W).astype(F32)
    n_freq = HEAD_DIM // 4
    inv = ROPE_THETA ** (-jnp.arange(n_freq, dtype=F32) / n_freq)
    ang_r = row[:, None] * inv[None, :]
    ang_c = col[:, None] * inv[None, :]
    cos = jnp.concatenate([jnp.cos(ang_r), jnp.cos(ang_r), jnp.cos(ang_c), jnp.cos(ang_c)], axis=-1)
    sin = jnp.concatenate([-jnp.sin(ang_r), jnp.sin(ang_r), -jnp.sin(ang_c), jnp.sin(ang_c)], axis=-1)
    return cos, sin


@jax.jit
def _forward(x_prompt, x_sample, cache_k, cache_v, state_gla, c, c_ctx, w_mod, b_mod, norm_w, w_in,
             q_norm_w, k_norm_w, conv_w, gla_a_up, gla_a_bias, gla_norm_w, w_out, final_norm_w):
    batch, seq, d = x_prompt.shape
    dec_batch, dec_seq, _ = x_sample.shape
    depth = w_in.shape[0]
    past = cache_k.shape[2]

    cond = jnp.zeros((8, d), F32).at[0].set(c_ctx).at[1:1 + dec_batch].set(c)
    mod3 = _modulation(cond, w_mod, b_mod).reshape(depth * 8, 1, 3 * d)
    rope_tables = _rope_tables(dec_seq)

    h = x_prompt.reshape(batch * seq, d)
    z = x_sample.reshape(dec_batch * dec_seq, d)
    zero_state = jnp.zeros((batch, 2, GLA_HEADS, GLA_DK, GLA_DV), F32)
    new_k, new_v, new_state = [], [], []
    for l in range(depth):
        w_in_l = jnp.pad(w_in[l].astype(BF16), ((0, 0), (0, IN_WIDTH_PADDED - IN_WIDTH)))
        w_out_l = w_out[l].astype(BF16)
        a_pad = jnp.zeros((2, LR_BLOCK, GLA_KW), F32)
        for zdir in range(2):
            a_pad = a_pad.at[zdir, zdir * GLA_RANK:(zdir + 1) * GLA_RANK].set(gla_a_up[l, zdir])
        a_pad = a_pad.astype(BF16)
        bias = gla_a_bias[l].reshape(2, 1, GLA_KW)
        final_w = final_norm_w if l == depth - 1 else None

        row_ctx = lambda tok, l=l: l * 8
        proj = _in_projection(h, mod3, norm_w[l], w_in_l, row_ctx)
        kn, kn_f32 = _key_norm(proj, k_norm_w[l], None, seq, True)
        att = _attention(proj, kn, q_norm_w[l], None, None, batch, seq)
        o_f, o_b, s_out = _gla(proj, a_pad, bias, zero_state, batch, seq)
        h = _out_projection(h, att, proj, o_f, o_b, conv_w[l], gla_norm_w[l], mod3, w_out_l, final_w,
                            row_ctx, seq)
        new_k.append(kn_f32.reshape(batch, seq, ATT_KV_HEADS, HEAD_DIM))
        new_v.append(proj[:, COL_V:COL_V + KV_WIDTH].astype(F32).reshape(batch, seq, ATT_KV_HEADS, HEAD_DIM))
        new_state.append(s_out)

        row_lat = lambda tok, l=l: l * 8 + 1 + tok // dec_seq
        proj = _in_projection(z, mod3, norm_w[l], w_in_l, row_lat)
        (kn,) = _key_norm(proj, k_norm_w[l], rope_tables, dec_seq, False)
        ctx_kv = (cache_k[:, l].reshape(dec_batch * past, KV_WIDTH).astype(BF16),
                  cache_v[:, l].reshape(dec_batch * past, KV_WIDTH).astype(BF16))
        att = _attention(proj, kn, q_norm_w[l], ctx_kv, rope_tables, dec_batch, dec_seq)
        o_f, o_b, _ = _gla(proj, a_pad, bias, state_gla[:, l], dec_batch, dec_seq)
        z = _out_projection(z, att, proj, o_f, o_b, conv_w[l], gla_norm_w[l], mod3, w_out_l, final_w,
                            row_lat, dec_seq)

    return (h.reshape(batch, seq, d), z.reshape(dec_batch, dec_seq, d),
            jnp.stack(new_k, axis=1), jnp.stack(new_v, axis=1), jnp.stack(new_state, axis=1))


def kernel(x_prompt, x_sample, cache_k, cache_v, state_gla, c, c_ctx, w_mod, b_mod, norm_w, w_in, q_norm_w,
           k_norm_w, conv_w, gla_a_up, gla_a_bias, gla_norm_w, w_out, final_norm_w):
    return _forward(x_prompt, x_sample, cache_k, cache_v, state_gla, c, c_ctx, w_mod, b_mod, norm_w, w_in,
                    q_norm_w, k_norm_w, conv_w, gla_a_up, gla_a_bias, gla_norm_w, w_out, final_norm_w)
```

```python
import functools

import jax
import jax.numpy as jnp
from jax import lax
from jax.experimental import pallas as pl
from jax.experimental.pallas import tpu as pltpu

F32 = jnp.float32
BF16 = jnp.bfloat16

GRID_W = 64
HEAD_DIM = 128
ATT_HEADS = 8
ATT_KV_HEADS = 2
HEADS_PER_KV = ATT_HEADS // ATT_KV_HEADS
ATT_WIDTH = ATT_HEADS * HEAD_DIM
KV_WIDTH = ATT_KV_HEADS * HEAD_DIM
GLA_HEADS = 4
GLA_DK = 64
GLA_DV = 128
GLA_KW = GLA_HEADS * GLA_DK
GLA_VW = GLA_HEADS * GLA_DV
GLA_RANK = 16
GLA_TAU = 16.0
ROPE_THETA = 10000.0
LOG2_E = 1.4426950408889634
EPS = 1e-6

COL_Q = 0
COL_K = 1024
COL_V = 1280
COL_AG = 1536
COL_CH = 2560
COL_CB = 3072
COL_CC = 3584
COL_CG = 4096
COL_GQK = 4608
COL_GV = 5120
COL_GG = 5632
COL_LR = 6144
IN_WIDTH = 6176
IN_TILE = 512
IN_MAIN = COL_LR
LR_BLOCK = 128

GLA_CHUNK = 64
GLA_SUB = 16
GLA_BLOCK = 256
GLA_EXP_CLAMP = 60.0

VMEM_LIMIT = 56 * 1024 * 1024


def _silu(x):
    return x * jax.nn.sigmoid(x)


def _rms(x, w):
    return x * lax.rsqrt(jnp.mean(x * x, axis=-1, keepdims=True) + EPS) * w


def _rope(y, cos, sin_signed):
    lane = lax.broadcasted_iota(jnp.int32, y.shape, 1)
    partner = jnp.where((lane & 63) < 32, pltpu.roll(y, 96, axis=1), pltpu.roll(y, 32, axis=1))
    return y * cos + partner * sin_signed


def _mod_kernel(c_ref, w_ref, b_ref, o_ref):
    a = _silu(c_ref[...]).astype(BF16)
    o_ref[0] = jnp.dot(a, w_ref[0].astype(BF16), preferred_element_type=F32) + b_ref[0]


def _modulation(cond, w_mod, b_mod):
    depth, d, n = w_mod.shape
    tn = 768
    return pl.pallas_call(
        _mod_kernel,
        grid=(depth, n // tn),
        in_specs=[pl.BlockSpec((8, d), lambda l, j: (0, 0)),
                  pl.BlockSpec((1, d, tn), lambda l, j: (l, 0, j)),
                  pl.BlockSpec((1, 1, tn), lambda l, j: (l, 0, j))],
        out_specs=pl.BlockSpec((1, 8, tn), lambda l, j: (l, 0, j)),
        out_shape=jax.ShapeDtypeStruct((depth, 8, n), F32),
        compiler_params=pltpu.CompilerParams(dimension_semantics=("parallel", "parallel"),
                                             vmem_limit_bytes=VMEM_LIMIT),
        name="modulation",
    )(cond, w_mod, b_mod.reshape(depth, 1, n))


_NT = (((1,), (1,)), ((), ()))


def _in_kernel(x_ref, nw_ref, shift_ref, scale_ref, w_ref, wlr_ref, o_ref, lr_ref, xn_ref):
    @pl.when(pl.program_id(1) == 0)
    def _():
        xn = _rms(x_ref[...], nw_ref[...]) * (1.0 + scale_ref[0]) + shift_ref[0]
        xn_ref[...] = xn.astype(BF16)
        lr_ref[...] = lax.dot_general(xn_ref[...], wlr_ref[0], _NT, preferred_element_type=F32).astype(BF16)

    o_ref[...] = lax.dot_general(xn_ref[...], w_ref[0], _NT, preferred_element_type=F32).astype(BF16)


def _in_projection(x, mod3, norm_w, w_in_t, w_lr_t, layer, mod_row):
    t, d = x.shape
    tm = 1024
    return pl.pallas_call(
        _in_kernel,
        grid=(t // tm, IN_MAIN // IN_TILE),
        in_specs=[pl.BlockSpec((tm, d), lambda i, j: (i, 0)),
                  pl.BlockSpec((1, d), lambda i, j: (0, 0)),
                  pl.BlockSpec((1, 1, d), lambda i, j: (mod_row(i * tm), 0, 0)),
                  pl.BlockSpec((1, 1, d), lambda i, j: (mod_row(i * tm), 0, 1)),
                  pl.BlockSpec((1, IN_TILE, d), lambda i, j: (layer, j, 0)),
                  pl.BlockSpec((1, LR_BLOCK, d), lambda i, j: (layer, 0, 0))],
        out_specs=[pl.BlockSpec((tm, IN_TILE), lambda i, j: (i, j)),
                   pl.BlockSpec((tm, LR_BLOCK), lambda i, j: (i, 0))],
        out_shape=[jax.ShapeDtypeStruct((t, IN_MAIN), BF16),
                   jax.ShapeDtypeStruct((t, LR_BLOCK), BF16)],
        scratch_shapes=[pltpu.VMEM((tm, d), BF16)],
        compiler_params=pltpu.CompilerParams(dimension_semantics=("parallel", "arbitrary"),
                                             vmem_limit_bytes=VMEM_LIMIT),
        name="in_projection",
    )(x, norm_w.reshape(1, d), mod3, mod3, w_in_t, w_lr_t)


def _kv_kernel(*refs, n_self, rope, has_ctx, emit_f32):
    kv_ref, kw_ref = refs[:2]
    pos = 2
    if rope:
        cos_ref, sin_ref = refs[pos:pos + 2]
        pos += 2
    if has_ctx:
        ck_ref, cv_ref = refs[pos:pos + 2]
        pos += 2
    o_ref = refs[pos]

    def own_rows():
        for g in range(ATT_KV_HEADS):
            cols = slice(g * HEAD_DIM, (g + 1) * HEAD_DIM)
            y = _rms(kv_ref[:, cols].astype(F32), kw_ref[...])
            if rope:
                y = _rope(y, cos_ref[...], sin_ref[...])
            o_ref[:, cols] = y.astype(BF16)
            if emit_f32:
                refs[pos + 1][:, cols] = y
        o_ref[:, KV_WIDTH:] = kv_ref[:, KV_WIDTH:]

    if has_ctx:
        pl.when(pl.program_id(1) < n_self)(own_rows)

        @pl.when(pl.program_id(1) >= n_self)
        def _():
            o_ref[:, :KV_WIDTH] = ck_ref[...].astype(BF16)
            o_ref[:, KV_WIDTH:] = cv_ref[...].astype(BF16)
    else:
        own_rows()


def _kv_prepare(proj, k_norm_w, rope_tables, ctx_kv, batch, seq_len, emit_f32):
    tr = min(512, seq_len)
    n_self = seq_len // tr
    rope = rope_tables is not None
    has_ctx = ctx_kv is not None
    assert not (has_ctx and emit_f32)
    n_ctx = ctx_kv[0].shape[0] // batch // tr if has_ctx else 0
    n_all = n_self + n_ctx
    own = lambda j: jnp.minimum(j, n_self - 1)
    in_specs = [pl.BlockSpec((tr, 2 * KV_WIDTH), lambda b, j: (b * n_self + own(j), COL_K // (2 * KV_WIDTH))),
                pl.BlockSpec((1, HEAD_DIM), lambda b, j: (0, 0))]
    args = [proj, k_norm_w.reshape(1, HEAD_DIM)]
    if rope:
        in_specs += [pl.BlockSpec((tr, HEAD_DIM), lambda b, j: (own(j), 0))] * 2
        args += list(rope_tables)
    if has_ctx:
        in_specs += [pl.BlockSpec((tr, KV_WIDTH), lambda b, j: (b * n_ctx + jnp.maximum(j - n_self, 0), 0))] * 2
        args += list(ctx_kv)
    out_specs = [pl.BlockSpec((tr, 2 * KV_WIDTH), lambda b, j: (b * n_all + j, 0))]
    out_shape = [jax.ShapeDtypeStruct((batch * n_all * tr, 2 * KV_WIDTH), BF16)]
    if emit_f32:
        out_specs.append(pl.BlockSpec((tr, KV_WIDTH), lambda b, j: (b * n_self + j, 0)))
        out_shape.append(jax.ShapeDtypeStruct((batch * seq_len, KV_WIDTH), F32))
    return pl.pallas_call(
        functools.partial(_kv_kernel, n_self=n_self, rope=rope, has_ctx=has_ctx, emit_f32=emit_f32),
        grid=(batch, n_all),
        in_specs=in_specs,
        out_specs=out_specs,
        out_shape=out_shape,
        compiler_params=pltpu.CompilerParams(dimension_semantics=("parallel", "parallel")),
        name="kv_prepare",
    )(*args)


def _attn_kernel(*refs, tq, tk, n_chunks, rope):
    q_ref, gate_ref, k_ref, v_ref, qw_ref = refs[:5]
    pos = 5
    if rope:
        cos_ref, sin_ref = refs[pos:pos + 2]
        pos += 2
    o_ref, qt_ref, s0_ref, s1_ref, m_ref, l_ref, acc_ref = refs[pos:pos + 7]

    q_scale = HEAD_DIM ** -0.5 * LOG2_E
    q = q_ref[...].astype(F32)
    for h in range(HEADS_PER_KV):
        y = _rms(q[:, h * HEAD_DIM:(h + 1) * HEAD_DIM], qw_ref[...])
        if rope:
            y = _rope(y, cos_ref[...], sin_ref[...])
        qt_ref[:, h * tq:(h + 1) * tq] = (y * q_scale).T.astype(BF16)
    m_ref[...] = jnp.full(m_ref.shape, -jnp.inf, F32)
    l_ref[...] = jnp.zeros(l_ref.shape, F32)
    acc_ref[...] = jnp.zeros(acc_ref.shape, F32)

    def rows(c):
        off = c * tk
        return pl.ds(off if isinstance(off, int) else pl.multiple_of(off, tk), tk)

    def scores(c, s_ref):
        s_ref[...] = jnp.dot(k_ref[rows(c), :], qt_ref[...], preferred_element_type=F32)

    def update(c, s_ref):
        s = s_ref[...]
        m_old = m_ref[...]
        m_new = jnp.maximum(m_old, jnp.max(s, axis=0, keepdims=True))
        p = jnp.exp2(s - m_new)
        alpha = jnp.exp2(m_old - m_new)
        l_ref[...] = alpha * l_ref[...] + jnp.sum(p, axis=0, keepdims=True)
        pv = lax.dot_general(v_ref[rows(c), :], p.astype(BF16), (((0,), (0,)), ((), ())),
                             preferred_element_type=F32)
        acc_ref[...] = alpha * acc_ref[...] + pv
        m_ref[...] = m_new

    scores(0, s0_ref)
    n_pairs = (n_chunks - 1) // 2

    def body(i, carry):
        c = 2 * i
        scores(c + 1, s1_ref)
        update(c, s0_ref)
        scores(c + 2, s0_ref)
        update(c + 1, s1_ref)
        return carry

    lax.fori_loop(0, n_pairs, body, 0)
    done = 2 * n_pairs
    if n_chunks - done == 2:
        scores(done + 1, s1_ref)
        update(done, s0_ref)
        update(done + 1, s1_ref)
    else:
        update(done, s0_ref)

    o = acc_ref[...] / l_ref[...]
    gate = gate_ref[...].astype(F32)
    for h in range(HEADS_PER_KV):
        cols = slice(h * HEAD_DIM, (h + 1) * HEAD_DIM)
        o_ref[:, cols] = (o[:, h * tq:(h + 1) * tq].T * _silu(gate[:, cols])).astype(BF16)


def _attention(proj, kv, q_norm_w, rope_tables, batch, seq_len):
    t = proj.shape[0]
    keys = kv.shape[0] // batch
    tq = 256
    tk = min(512, keys)
    nq = seq_len // tq
    gw = HEADS_PER_KV * HEAD_DIM
    rope = rope_tables is not None
    in_specs = [pl.BlockSpec((tq, gw), lambda b, g, i: (b * nq + i, COL_Q // gw + g)),
                pl.BlockSpec((tq, gw), lambda b, g, i: (b * nq + i, COL_AG // gw + g)),
                pl.BlockSpec((keys, HEAD_DIM), lambda b, g, i: (b, g)),
                pl.BlockSpec((keys, HEAD_DIM), lambda b, g, i: (b, ATT_KV_HEADS + g)),
                pl.BlockSpec((1, HEAD_DIM), lambda b, g, i: (0, 0))]
    args = [proj, proj, kv, kv, q_norm_w.reshape(1, HEAD_DIM)]
    if rope:
        in_specs += [pl.BlockSpec((tq, HEAD_DIM), lambda b, g, i: (i, 0))] * 2
        args += list(rope_tables)
    return pl.pallas_call(
        functools.partial(_attn_kernel, tq=tq, tk=tk, n_chunks=keys // tk, rope=rope),
        grid=(batch, ATT_KV_HEADS, nq),
        in_specs=in_specs,
        out_specs=pl.BlockSpec((tq, gw), lambda b, g, i: (b * nq + i, g)),
        out_shape=jax.ShapeDtypeStruct((t, ATT_WIDTH), BF16),
        scratch_shapes=[pltpu.VMEM((HEAD_DIM, HEADS_PER_KV * tq), BF16),
                        pltpu.VMEM((tk, HEADS_PER_KV * tq), F32),
                        pltpu.VMEM((tk, HEADS_PER_KV * tq), F32),
                        pltpu.VMEM((1, HEADS_PER_KV * tq), F32),
                        pltpu.VMEM((1, HEADS_PER_KV * tq), F32),
                        pltpu.VMEM((HEAD_DIM, HEADS_PER_KV * tq), F32)],
        compiler_params=pltpu.CompilerParams(dimension_semantics=("parallel", "parallel", "arbitrary"),
                                             vmem_limit_bytes=VMEM_LIMIT),
        name="attention",
    )(*args)


def _log_sigmoid(x):
    return -(jnp.maximum(-x, 0.0) + jnp.log1p(jnp.exp(-jnp.abs(x))))


def _gla_chunk(qk, v, lr, a, bias, st_ref, o_ref, rows, rev):
    c, s = GLA_CHUNK, GLA_SUB
    q = qk[:, :GLA_KW].astype(F32) * GLA_DK ** -0.5
    k = qk[:, GLA_KW:].astype(F32)
    logits = jnp.dot(lr, a, preferred_element_type=F32) + bias
    g = _log_sigmoid(logits) * (1.0 / GLA_TAU)

    g1 = g.astype(BF16)
    r1 = g - g1.astype(F32)
    g2 = r1.astype(BF16)
    g3 = (r1 - g2.astype(F32)).astype(BF16)
    ti = lax.broadcasted_iota(jnp.int32, (c, c), 0)
    si = lax.broadcasted_iota(jnp.int32, (c, c), 1)
    causal = (ti <= si) if rev else (ti >= si)
    tri = jnp.where(causal, 1.0, 0.0).astype(BF16)
    b = (jnp.dot(tri, g1, preferred_element_type=F32) + jnp.dot(tri, g2, preferred_element_type=F32)
         + jnp.dot(tri, g3, preferred_element_type=F32))

    b_total = b[0:1] if rev else b[c - 1:c]
    q_inter = (q * jnp.exp(b)).astype(BF16)
    k_state = (k * jnp.exp(b_total - b)).astype(BF16)
    decay_total = jnp.exp(b_total)

    srow = lax.broadcasted_iota(jnp.int32, (c, GLA_KW), 0)
    pairs = []
    for i in range(c // s):
        lo, hi = i * s, (i + 1) * s
        if rev:
            ref = b[hi:hi + 1] if hi < c else jnp.zeros((1, GLA_KW), F32)
            valid = srow >= lo
        else:
            ref = b[lo - 1:lo] if lo > 0 else jnp.zeros((1, GLA_KW), F32)
            valid = srow < hi
        q_hat = (q[lo:hi] * jnp.exp(b[lo:hi] - ref)).astype(BF16)
        expo = jnp.where(valid, jnp.minimum(ref - b, GLA_EXP_CLAMP), -jnp.inf)
        k_hat = (k * jnp.exp(expo)).astype(BF16)
        pairs.append((q_hat, k_hat))

    for h in range(GLA_HEADS):
        kc = slice(h * GLA_DK, (h + 1) * GLA_DK)
        vc = slice(h * GLA_DV, (h + 1) * GLA_DV)
        att = jnp.concatenate(
            [lax.dot_general(qh[:, kc], kh[:, kc], (((1,), (1,)), ((), ())), preferred_element_type=F32)
             for qh, kh in pairs], axis=0)
        att = jnp.where(causal, att, 0.0).astype(BF16)
        vh = v[:, vc]
        st = st_ref[h]
        o = jnp.dot(att, vh, preferred_element_type=F32) + lax.dot_general(
            q_inter[:, kc], st.astype(BF16), (((1,), (1,)), ((), ())), preferred_element_type=F32)
        o_ref[rows, vc] = o.astype(BF16)
        st_ref[h] = st * decay_total[:, kc] + lax.dot_general(
            vh, k_state[:, kc], (((0,), (0,)), ((), ())), preferred_element_type=F32)


def _gla_kernel(qkf_ref, vf_ref, lrf_ref, qkb_ref, vb_ref, lrb_ref, a_ref, bias_ref, s0_ref,
                of_ref, ob_ref, so_ref, st_ref):
    j = pl.program_id(1)

    @pl.when(j == 0)
    def _():
        for z in range(2):
            for h in range(GLA_HEADS):
                st_ref[z, h] = s0_ref[0, z, h].T

    n_chunks = GLA_BLOCK // GLA_CHUNK
    for c in range(n_chunks):
        rows = slice(c * GLA_CHUNK, (c + 1) * GLA_CHUNK)
        _gla_chunk(qkf_ref[rows, :], vf_ref[rows, :], lrf_ref[rows, :], a_ref[0], bias_ref[0],
                   st_ref.at[0], of_ref, rows, rev=False)
        rows = slice((n_chunks - 1 - c) * GLA_CHUNK, (n_chunks - c) * GLA_CHUNK)
        _gla_chunk(qkb_ref[rows, :], vb_ref[rows, :], lrb_ref[rows, :], a_ref[1], bias_ref[1],
                   st_ref.at[1], ob_ref, rows, rev=True)

    @pl.when(j == pl.num_programs(1) - 1)
    def _():
        for z in range(2):
            for h in range(GLA_HEADS):
                so_ref[0, z, h] = st_ref[z, h].T


def _gla(proj, lr, a_pad, bias, s0, batch, seq_len):
    t = proj.shape[0]
    nb = seq_len // GLA_BLOCK
    fwd = lambda b, j: b * nb + j
    bwd = lambda b, j: b * nb + nb - 1 - j
    qk_w = 2 * GLA_KW

    def proj_specs(row):
        return [pl.BlockSpec((GLA_BLOCK, qk_w), lambda b, j: (row(b, j), COL_GQK // qk_w)),
                pl.BlockSpec((GLA_BLOCK, GLA_VW), lambda b, j: (row(b, j), COL_GV // GLA_VW)),
                pl.BlockSpec((GLA_BLOCK, LR_BLOCK), lambda b, j: (row(b, j), 0))]

    state_spec = pl.BlockSpec((1, 2, GLA_HEADS, GLA_DK, GLA_DV), lambda b, j: (b, 0, 0, 0, 0))
    return pl.pallas_call(
        _gla_kernel,
        grid=(batch, nb),
        in_specs=proj_specs(fwd) + proj_specs(bwd) + [
            pl.BlockSpec((2, LR_BLOCK, GLA_KW), lambda b, j: (0, 0, 0)),
            pl.BlockSpec((2, 1, GLA_KW), lambda b, j: (0, 0, 0)),
            state_spec],
        out_specs=[pl.BlockSpec((GLA_BLOCK, GLA_VW), lambda b, j: (fwd(b, j), 0)),
                   pl.BlockSpec((GLA_BLOCK, GLA_VW), lambda b, j: (bwd(b, j), 0)),
                   state_spec],
        out_shape=[jax.ShapeDtypeStruct((t, GLA_VW), BF16),
                   jax.ShapeDtypeStruct((t, GLA_VW), BF16),
                   jax.ShapeDtypeStruct((batch, 2, GLA_HEADS, GLA_DK, GLA_DV), F32)],
        scratch_shapes=[pltpu.VMEM((2, GLA_HEADS, GLA_DV, GLA_DK), F32)],
        compiler_params=pltpu.CompilerParams(dimension_semantics=("parallel", "arbitrary"),
                                             vmem_limit_bytes=VMEM_LIMIT),
        name="gla",
    )(proj, proj, lr, proj, proj, lr, a_pad, bias, s0)


HALO = 16


def _out_kernel(*refs, tm, seq_len, final):
    (x_ref, a_ref, ch_ref, cb_ref, cc_ref, cg_ref, chp_ref, ccp_ref, chn_ref, ccn_ref,
     gg_ref, of_ref, ob_ref, cw_ref, gnw_ref, gate_ref, w_ref) = refs[:17]
    pos = 17
    if final:
        fnw_ref = refs[pos]
        pos += 1
    o_ref, mix_ref = refs[pos:pos + 2]

    mix_ref[:, :ATT_WIDTH] = a_ref[...]

    u = cc_ref[...].astype(F32) * ch_ref[...].astype(F32)
    u_before = ccp_ref[HALO - 1:HALO, :].astype(F32) * chp_ref[HALO - 1:HALO, :].astype(F32)
    u_after = ccn_ref[0:1, :].astype(F32) * chn_ref[0:1, :].astype(F32)
    row = lax.broadcasted_iota(jnp.int32, u.shape, 0)
    seq_pos = (pl.program_id(0) * tm + row) & (seq_len - 1)
    u_prev = jnp.where(row == 0, u_before, pltpu.roll(u, 1, axis=0))
    u_prev = jnp.where(seq_pos == 0, 0.0, u_prev)
    u_next = jnp.where(row == tm - 1, u_after, pltpu.roll(u, tm - 1, axis=0))
    u_next = jnp.where(seq_pos == seq_len - 1, 0.0, u_next)
    conv = u_prev * cw_ref[0:1, :] + u * cw_ref[1:2, :] + u_next * cw_ref[2:3, :]
    out_b = cb_ref[...].astype(F32) * conv * _silu(cg_ref[...].astype(F32))
    mix_ref[:, ATT_WIDTH:ATT_WIDTH + out_b.shape[1]] = out_b.astype(BF16)

    o = of_ref[...].astype(F32) + ob_ref[...].astype(F32)
    gg = gg_ref[...].astype(F32)
    base = mix_ref.shape[1] - GLA_VW
    for h in range(GLA_HEADS):
        cols = slice(h * GLA_DV, (h + 1) * GLA_DV)
        y = _rms(o[:, cols], gnw_ref[...]) * _silu(gg[:, cols])
        mix_ref[:, base + h * GLA_DV:base + (h + 1) * GLA_DV] = y.astype(BF16)

    r = jnp.dot(mix_ref[...], w_ref[0], preferred_element_type=F32)
    xn = x_ref[...] + gate_ref[0] * r
    if final:
        xn = _rms(xn, fnw_ref[...])
    o_ref[...] = xn


def _out_projection(x, att, proj, o_f, o_b, conv_w, gla_norm_w, mod3, w_out, layer, final_norm_w, mod_row,
                    seq_len):
    t, d = x.shape
    tm = 256
    cw = conv_w.shape[1]
    final = final_norm_w is not None
    per_halo = tm // HALO
    last_halo = t // HALO - 1

    def col(c, width):
        return lambda i: (i, c // width)

    prev = lambda c: (lambda i: (jnp.maximum(i * per_halo - 1, 0), c // cw))
    nxt = lambda c: (lambda i: (jnp.minimum((i + 1) * per_halo, last_halo), c // cw))
    in_specs = [pl.BlockSpec((tm, d), lambda i: (i, 0)),
                pl.BlockSpec((tm, ATT_WIDTH), lambda i: (i, 0)),
                pl.BlockSpec((tm, cw), col(COL_CH, cw)),
                pl.BlockSpec((tm, cw), col(COL_CB, cw)),
                pl.BlockSpec((tm, cw), col(COL_CC, cw)),
                pl.BlockSpec((tm, cw), col(COL_CG, cw)),
                pl.BlockSpec((HALO, cw), prev(COL_CH)),
                pl.BlockSpec((HALO, cw), prev(COL_CC)),
                pl.BlockSpec((HALO, cw), nxt(COL_CH)),
                pl.BlockSpec((HALO, cw), nxt(COL_CC)),
                pl.BlockSpec((tm, GLA_VW), col(COL_GG, GLA_VW)),
                pl.BlockSpec((tm, GLA_VW), lambda i: (i, 0)),
                pl.BlockSpec((tm, GLA_VW), lambda i: (i, 0)),
                pl.BlockSpec((3, cw), lambda i: (0, 0)),
                pl.BlockSpec((1, GLA_DV), lambda i: (0, 0)),
                pl.BlockSpec((1, 1, d), lambda i: (mod_row(i * tm), 0, 2)),
                pl.BlockSpec((1,) + w_out.shape[1:], lambda i: (layer, 0, 0))]
    args = [x, att, proj, proj, proj, proj, proj, proj, proj, proj, proj, o_f, o_b,
            conv_w, gla_norm_w.reshape(1, GLA_DV), mod3, w_out]
    if final:
        in_specs.append(pl.BlockSpec((1, d), lambda i: (0, 0)))
        args.append(final_norm_w.reshape(1, d))
    return pl.pallas_call(
        functools.partial(_out_kernel, tm=tm, seq_len=seq_len, final=final),
        grid=(t // tm,),
        in_specs=in_specs,
        out_specs=pl.BlockSpec((tm, d), lambda i: (i, 0)),
        out_shape=jax.ShapeDtypeStruct((t, d), F32),
        scratch_shapes=[pltpu.VMEM((tm, w_out.shape[1]), BF16)],
        compiler_params=pltpu.CompilerParams(dimension_semantics=("parallel",),
                                             vmem_limit_bytes=VMEM_LIMIT),
        name="out_projection",
    )(*args)


def _rope_tables(seq_len):
    pos = jnp.arange(seq_len)
    row = (pos // GRID_W).astype(F32)
    col = (pos % GRID_W).astype(F32)
    n_freq = HEAD_DIM // 4
    inv = ROPE_THETA ** (-jnp.arange(n_freq, dtype=F32) / n_freq)
    ang_r = row[:, None] * inv[None, :]
    ang_c = col[:, None] * inv[None, :]
    cos = jnp.concatenate([jnp.cos(ang_r), jnp.cos(ang_r), jnp.cos(ang_c), jnp.cos(ang_c)], axis=-1)
    sin = jnp.concatenate([-jnp.sin(ang_r), jnp.sin(ang_r), -jnp.sin(ang_c), jnp.sin(ang_c)], axis=-1)
    return cos, sin


@jax.jit
def _forward(x_prompt, x_sample, cache_k, cache_v, state_gla, c, c_ctx, w_mod, b_mod, norm_w, w_in,
             q_norm_w, k_norm_w, conv_w, gla_a_up, gla_a_bias, gla_norm_w, w_out, final_norm_w):
    batch, seq, d = x_prompt.shape
    dec_batch, dec_seq, _ = x_sample.shape
    depth = w_in.shape[0]
    past = cache_k.shape[2]

    cond = jnp.zeros((8, d), F32).at[0].set(c_ctx).at[1:1 + dec_batch].set(c)
    mod3 = _modulation(cond, w_mod, b_mod).reshape(depth * 8, 1, 3 * d)
    rope_tables = _rope_tables(dec_seq)

    h = x_prompt.reshape(batch * seq, d)
    z = x_sample.reshape(dec_batch * dec_seq, d)
    zero_state = jnp.zeros((batch, 2, GLA_HEADS, GLA_DK, GLA_DV), F32)
    w_in_t = jnp.swapaxes(w_in, 1, 2).astype(BF16)
    w_lr_t = jnp.pad(w_in_t[:, COL_LR:, :], ((0, 0), (0, LR_BLOCK - (IN_WIDTH - COL_LR)), (0, 0)))
    w_out_b = w_out.astype(BF16)
    new_k, new_v, new_state = [], [], []
    for l in range(depth):
        a_pad = jnp.zeros((2, LR_BLOCK, GLA_KW), F32)
        for zdir in range(2):
            a_pad = a_pad.at[zdir, zdir * GLA_RANK:(zdir + 1) * GLA_RANK].set(gla_a_up[l, zdir])
        a_pad = a_pad.astype(BF16)
        bias = gla_a_bias[l].reshape(2, 1, GLA_KW)
        final_w = final_norm_w if l == depth - 1 else None

        row_ctx = lambda tok, l=l: l * 8
        proj, lr = _in_projection(h, mod3, norm_w[l], w_in_t, w_lr_t, l, row_ctx)
        kv, kn_f32 = _kv_prepare(proj, k_norm_w[l], None, None, batch, seq, True)
        att = _attention(proj, kv, q_norm_w[l], None, batch, seq)
        o_f, o_b, s_out = _gla(proj, lr, a_pad, bias, zero_state, batch, seq)
        h = _out_projection(h, att, proj, o_f, o_b, conv_w[l], gla_norm_w[l], mod3, w_out_b, l, final_w,
                            row_ctx, seq)
        new_k.append(kn_f32.reshape(batch, seq, ATT_KV_HEADS, HEAD_DIM))
        new_v.append(proj[:, COL_V:COL_V + KV_WIDTH].astype(F32).reshape(batch, seq, ATT_KV_HEADS, HEAD_DIM))
        new_state.append(s_out)

        row_lat = lambda tok, l=l: l * 8 + 1 + tok // dec_seq
        proj, lr = _in_projection(z, mod3, norm_w[l], w_in_t, w_lr_t, l, row_lat)
        ctx_kv = (cache_k[:, l].reshape(dec_batch * past, KV_WIDTH),
                  cache_v[:, l].reshape(dec_batch * past, KV_WIDTH))
        (kv,) = _kv_prepare(proj, k_norm_w[l], rope_tables, ctx_kv, dec_batch, dec_seq, False)
        att = _attention(proj, kv, q_norm_w[l], rope_tables, dec_batch, dec_seq)
        o_f, o_b, _ = _gla(proj, lr, a_pad, bias, state_gla[:, l], dec_batch, dec_seq)
        z = _out_projection(z, att, proj, o_f, o_b, conv_w[l], gla_norm_w[l], mod3, w_out_b, l, final_w,
                            row_lat, dec_seq)

    return (h.reshape(batch, seq, d), z.reshape(dec_batch, dec_seq, d),
            jnp.stack(new_k, axis=1), jnp.stack(new_v, axis=1), jnp.stack(new_state, axis=1))


def kernel(x_prompt, x_sample, cache_k, cache_v, state_gla, c, c_ctx, w_mod, b_mod, norm_w, w_in, q_norm_w,
           k_norm_w, conv_w, gla_a_up, gla_a_bias, gla_norm_w, w_out, final_norm_w):
    return _forward(x_prompt, x_sample, cache_k, cache_v, state_gla, c, c_ctx, w_mod, b_mod, norm_w, w_in,
                    q_norm_w, k_norm_w, conv_w, gla_a_up, gla_a_bias, gla_norm_w, w_out, final_norm_w)
```

```python
import functools

import numpy as np
import jax
import jax.numpy as jnp
from jax import lax
from jax.experimental import pallas as pl
from jax.experimental.pallas import tpu as pltpu

F32 = jnp.float32
BF16 = jnp.bfloat16

GRID_W = 64
HEAD_DIM = 128
ATT_HEADS = 8
ATT_KV_HEADS = 2
HEADS_PER_KV = ATT_HEADS // ATT_KV_HEADS
ATT_WIDTH = ATT_HEADS * HEAD_DIM
KV_WIDTH = ATT_KV_HEADS * HEAD_DIM
GLA_HEADS = 4
GLA_DK = 64
GLA_DV = 128
GLA_KW = GLA_HEADS * GLA_DK
GLA_VW = GLA_HEADS * GLA_DV
GLA_RANK = 16
GLA_TAU = 16.0
ROPE_THETA = 10000.0
LOG2_E = 1.4426950408889634
EPS = 1e-6

COL_Q = 0
COL_K = 1024
COL_V = 1280
COL_AG = 1536
COL_CH = 2560
COL_CB = 3072
COL_CC = 3584
COL_CG = 4096
COL_GQK = 4608
COL_GV = 5120
COL_GG = 5632
COL_LR = 6144
IN_WIDTH = 6176
IN_TILE = 1024
IN_ROW_CHUNKS = 4
IN_MAIN = COL_LR
LR_BLOCK = 128

GLA_CHUNK = 128
GLA_LEVELS = (16, 32, 64, 128)
GLA_BLOCK = 256
GLA_EXP_CLAMP = 60.0

VMEM_LIMIT = 56 * 1024 * 1024


def _silu(x):
    return x * jax.nn.sigmoid(x)


def _rms(x, w):
    return x * lax.rsqrt(jnp.mean(x * x, axis=-1, keepdims=True) + EPS) * w


def _rope(y, cos, sin_signed):
    lane = lax.broadcasted_iota(jnp.int32, y.shape, 1)
    partner = jnp.where((lane & 63) < 32, pltpu.roll(y, 96, axis=1), pltpu.roll(y, 32, axis=1))
    return y * cos + partner * sin_signed


def _mod_kernel(c_ref, w_ref, b_ref, o_ref):
    a = _silu(c_ref[...]).astype(BF16)
    o_ref[0] = jnp.dot(a, w_ref[0].astype(BF16), preferred_element_type=F32) + b_ref[0]


def _modulation(cond, w_mod, b_mod):
    depth, d, n = w_mod.shape
    tn = 768
    return pl.pallas_call(
        _mod_kernel,
        grid=(depth, n // tn),
        in_specs=[pl.BlockSpec((8, d), lambda l, j: (0, 0)),
                  pl.BlockSpec((1, d, tn), lambda l, j: (l, 0, j)),
                  pl.BlockSpec((1, 1, tn), lambda l, j: (l, 0, j))],
        out_specs=pl.BlockSpec((1, 8, tn), lambda l, j: (l, 0, j)),
        out_shape=jax.ShapeDtypeStruct((depth, 8, n), F32),
        compiler_params=pltpu.CompilerParams(dimension_semantics=("parallel", "parallel"),
                                             vmem_limit_bytes=VMEM_LIMIT),
        name="modulation",
    )(cond, w_mod, b_mod.reshape(depth, 1, n))


_NT = (((1,), (1,)), ((), ()))


def _in_kernel(x_ref, nw_ref, shift_ref, scale_ref, w_ref, wlr_ref, o_ref, lr_ref, xn_ref):
    def project(xn, w):
        return lax.dot_general(xn, w, _NT, preferred_element_type=F32).astype(BF16)

    @pl.when(pl.program_id(1) == 0)
    def _():
        gain = nw_ref[...] * (1.0 + scale_ref[0])
        per = x_ref.shape[0] // IN_ROW_CHUNKS
        for r in range(IN_ROW_CHUNKS):
            rows = slice(r * per, (r + 1) * per)
            x = x_ref[rows, :]
            inv = lax.rsqrt(jnp.mean(x * x, axis=-1, keepdims=True) + EPS)
            xn = (x * inv * gain + shift_ref[0]).astype(BF16)
            xn_ref[rows, :] = xn
            o_ref[rows, :] = project(xn, w_ref[0])
            lr_ref[rows, :] = project(xn, wlr_ref[0])

    @pl.when(pl.program_id(1) > 0)
    def _():
        o_ref[...] = project(xn_ref[...], w_ref[0])


def _in_projection(x, mod3, norm_w, w_in_t, w_lr_t, layer, mod_row):
    t, d = x.shape
    tm = 1024
    return pl.pallas_call(
        _in_kernel,
        grid=(t // tm, IN_MAIN // IN_TILE),
        in_specs=[pl.BlockSpec((tm, d), lambda i, j: (i, 0)),
                  pl.BlockSpec((1, d), lambda i, j: (0, 0)),
                  pl.BlockSpec((1, 1, d), lambda i, j: (mod_row(i * tm), 0, 0)),
                  pl.BlockSpec((1, 1, d), lambda i, j: (mod_row(i * tm), 0, 1)),
                  pl.BlockSpec((1, IN_TILE, d), lambda i, j: (layer, j, 0)),
                  pl.BlockSpec((1, LR_BLOCK, d), lambda i, j: (layer, 0, 0))],
        out_specs=[pl.BlockSpec((tm, IN_TILE), lambda i, j: (i, j)),
                   pl.BlockSpec((tm, LR_BLOCK), lambda i, j: (i, 0))],
        out_shape=[jax.ShapeDtypeStruct((t, IN_MAIN), BF16),
                   jax.ShapeDtypeStruct((t, LR_BLOCK), BF16)],
        scratch_shapes=[pltpu.VMEM((tm, d), BF16)],
        compiler_params=pltpu.CompilerParams(dimension_semantics=("parallel", "arbitrary"),
                                             vmem_limit_bytes=VMEM_LIMIT),
        name="in_projection",
    )(x, norm_w.reshape(1, d), mod3, mod3, w_in_t, w_lr_t)


def _kv_kernel(*refs, n_self, rope, has_ctx, emit_f32):
    kv_ref, kw_ref = refs[:2]
    pos = 2
    if rope:
        cos_ref, sin_ref = refs[pos:pos + 2]
        pos += 2
    if has_ctx:
        ck_ref, cv_ref = refs[pos:pos + 2]
        pos += 2
    o_ref = refs[pos]

    def own_rows():
        for g in range(ATT_KV_HEADS):
            cols = slice(g * HEAD_DIM, (g + 1) * HEAD_DIM)
            y = _rms(kv_ref[:, cols].astype(F32), kw_ref[...])
            if rope:
                y = _rope(y, cos_ref[...], sin_ref[...])
            o_ref[:, cols] = y.astype(BF16)
            if emit_f32:
                refs[pos + 1][:, cols] = y
        o_ref[:, KV_WIDTH:] = kv_ref[:, KV_WIDTH:]

    if has_ctx:
        pl.when(pl.program_id(1) < n_self)(own_rows)

        @pl.when(pl.program_id(1) >= n_self)
        def _():
            o_ref[:, :KV_WIDTH] = ck_ref[...].astype(BF16)
            o_ref[:, KV_WIDTH:] = cv_ref[...].astype(BF16)
    else:
        own_rows()


def _kv_prepare(proj, k_norm_w, rope_tables, ctx_kv, batch, seq_len, emit_f32):
    tr = min(512, seq_len)
    n_self = seq_len // tr
    rope = rope_tables is not None
    has_ctx = ctx_kv is not None
    assert not (has_ctx and emit_f32)
    n_ctx = ctx_kv[0].shape[0] // batch // tr if has_ctx else 0
    n_all = n_self + n_ctx
    own = lambda j: jnp.minimum(j, n_self - 1)
    in_specs = [pl.BlockSpec((tr, 2 * KV_WIDTH), lambda b, j: (b * n_self + own(j), COL_K // (2 * KV_WIDTH))),
                pl.BlockSpec((1, HEAD_DIM), lambda b, j: (0, 0))]
    args = [proj, k_norm_w.reshape(1, HEAD_DIM)]
    if rope:
        in_specs += [pl.BlockSpec((tr, HEAD_DIM), lambda b, j: (own(j), 0))] * 2
        args += list(rope_tables)
    if has_ctx:
        in_specs += [pl.BlockSpec((tr, KV_WIDTH), lambda b, j: (b * n_ctx + jnp.maximum(j - n_self, 0), 0))] * 2
        args += list(ctx_kv)
    out_specs = [pl.BlockSpec((tr, 2 * KV_WIDTH), lambda b, j: (b * n_all + j, 0))]
    out_shape = [jax.ShapeDtypeStruct((batch * n_all * tr, 2 * KV_WIDTH), BF16)]
    if emit_f32:
        out_specs.append(pl.BlockSpec((tr, KV_WIDTH), lambda b, j: (b * n_self + j, 0)))
        out_shape.append(jax.ShapeDtypeStruct((batch * seq_len, KV_WIDTH), F32))
    return pl.pallas_call(
        functools.partial(_kv_kernel, n_self=n_self, rope=rope, has_ctx=has_ctx, emit_f32=emit_f32),
        grid=(batch, n_all),
        in_specs=in_specs,
        out_specs=out_specs,
        out_shape=out_shape,
        compiler_params=pltpu.CompilerParams(dimension_semantics=("parallel", "parallel")),
        name="kv_prepare",
    )(*args)


def _attn_kernel(*refs, tq, tk, n_chunks, rope):
    q_ref, gate_ref, k_ref, v_ref, qw_ref = refs[:5]
    pos = 5
    if rope:
        cos_ref, sin_ref = refs[pos:pos + 2]
        pos += 2
    o_ref, qt_ref, s0_ref, s1_ref, m_ref, l_ref, acc_ref = refs[pos:pos + 7]

    q_scale = HEAD_DIM ** -0.5 * LOG2_E
    q = q_ref[...].astype(F32)
    for h in range(HEADS_PER_KV):
        y = _rms(q[:, h * HEAD_DIM:(h + 1) * HEAD_DIM], qw_ref[...])
        if rope:
            y = _rope(y, cos_ref[...], sin_ref[...])
        qt_ref[:, h * tq:(h + 1) * tq] = (y * q_scale).T.astype(BF16)
    m_ref[...] = jnp.full(m_ref.shape, -jnp.inf, F32)
    l_ref[...] = jnp.zeros(l_ref.shape, F32)
    acc_ref[...] = jnp.zeros(acc_ref.shape, F32)

    def rows(c):
        off = c * tk
        return pl.ds(off if isinstance(off, int) else pl.multiple_of(off, tk), tk)

    def scores(c, s_ref):
        s_ref[...] = jnp.dot(k_ref[rows(c), :], qt_ref[...], preferred_element_type=F32)

    def update(c, s_ref):
        s = s_ref[...]
        m_old = m_ref[...]
        m_new = jnp.maximum(m_old, jnp.max(s, axis=0, keepdims=True))
        p = jnp.exp2(s - m_new)
        alpha = jnp.exp2(m_old - m_new)
        l_ref[...] = alpha * l_ref[...] + jnp.sum(p, axis=0, keepdims=True)
        pv = lax.dot_general(v_ref[rows(c), :], p.astype(BF16), (((0,), (0,)), ((), ())),
                             preferred_element_type=F32)
        acc_ref[...] = alpha * acc_ref[...] + pv
        m_ref[...] = m_new

    scores(0, s0_ref)
    n_pairs = (n_chunks - 1) // 2

    def body(i, carry):
        c = 2 * i
        scores(c + 1, s1_ref)
        update(c, s0_ref)
        scores(c + 2, s0_ref)
        update(c + 1, s1_ref)
        return carry

    lax.fori_loop(0, n_pairs, body, 0)
    done = 2 * n_pairs
    if n_chunks - done == 2:
        scores(done + 1, s1_ref)
        update(done, s0_ref)
        update(done + 1, s1_ref)
    else:
        update(done, s0_ref)

    o = acc_ref[...] / l_ref[...]
    gate = gate_ref[...].astype(F32)
    for h in range(HEADS_PER_KV):
        cols = slice(h * HEAD_DIM, (h + 1) * HEAD_DIM)
        o_ref[:, cols] = (o[:, h * tq:(h + 1) * tq].T * _silu(gate[:, cols])).astype(BF16)


def _attention(proj, kv, q_norm_w, rope_tables, batch, seq_len):
    t = proj.shape[0]
    keys = kv.shape[0] // batch
    tq = 256
    tk = min(512, keys)
    nq = seq_len // tq
    gw = HEADS_PER_KV * HEAD_DIM
    rope = rope_tables is not None
    in_specs = [pl.BlockSpec((tq, gw), lambda b, g, i: (b * nq + i, COL_Q // gw + g)),
                pl.BlockSpec((tq, gw), lambda b, g, i: (b * nq + i, COL_AG // gw + g)),
                pl.BlockSpec((keys, HEAD_DIM), lambda b, g, i: (b, g)),
                pl.BlockSpec((keys, HEAD_DIM), lambda b, g, i: (b, ATT_KV_HEADS + g)),
                pl.BlockSpec((1, HEAD_DIM), lambda b, g, i: (0, 0))]
    args = [proj, proj, kv, kv, q_norm_w.reshape(1, HEAD_DIM)]
    if rope:
        in_specs += [pl.BlockSpec((tq, HEAD_DIM), lambda b, g, i: (i, 0))] * 2
        args += list(rope_tables)
    return pl.pallas_call(
        functools.partial(_attn_kernel, tq=tq, tk=tk, n_chunks=keys // tk, rope=rope),
        grid=(batch, ATT_KV_HEADS, nq),
        in_specs=in_specs,
        out_specs=pl.BlockSpec((tq, gw), lambda b, g, i: (b * nq + i, g)),
        out_shape=jax.ShapeDtypeStruct((t, ATT_WIDTH), BF16),
        scratch_shapes=[pltpu.VMEM((HEAD_DIM, HEADS_PER_KV * tq), BF16),
                        pltpu.VMEM((tk, HEADS_PER_KV * tq), F32),
                        pltpu.VMEM((tk, HEADS_PER_KV * tq), F32),
                        pltpu.VMEM((1, HEADS_PER_KV * tq), F32),
                        pltpu.VMEM((1, HEADS_PER_KV * tq), F32),
                        pltpu.VMEM((HEAD_DIM, HEADS_PER_KV * tq), F32)],
        compiler_params=pltpu.CompilerParams(dimension_semantics=("parallel", "parallel", "arbitrary"),
                                             vmem_limit_bytes=VMEM_LIMIT),
        name="attention",
    )(*args)


def _log_sigmoid(x):
    return -(jnp.maximum(-x, 0.0) + jnp.log(1.0 + jnp.exp(-jnp.abs(x))))


def _gla_level_table():
    t = np.arange(GLA_CHUNK)[:, None]
    s = np.arange(GLA_CHUNK)[None, :]
    lvl = np.full((GLA_CHUNK, GLA_CHUNK), len(GLA_LEVELS), np.int32)
    for i, size in reversed(list(enumerate(GLA_LEVELS))):
        lvl = np.where(t // size == s // size, i, lvl)
    fwd = np.where(s <= t, lvl, len(GLA_LEVELS)).astype(np.int32)
    return jnp.asarray(np.stack([fwd, fwd.T]))


def _stack_rows(b, picks, size):
    width = b.shape[1]
    return jnp.concatenate(
        [jnp.zeros((size, width), F32) if p is None else jnp.broadcast_to(b[p:p + 1], (size, width))
         for p in picks], axis=0)


def _gla_decays(lr, a, bias):
    logits = jnp.dot(lr, a, preferred_element_type=F32) + bias
    return _log_sigmoid(logits) * (1.0 / GLA_TAU)


def _gla_cumsum(g, lvl):
    g_hi = g.astype(BF16)
    g_lo = (g - g_hi.astype(F32)).astype(BF16)
    tri = jnp.where(lvl < len(GLA_LEVELS), 1.0, 0.0).astype(BF16)
    return jnp.dot(tri, g_hi, preferred_element_type=F32) + jnp.dot(tri, g_lo, preferred_element_type=F32)


def _gla_operands(qk, b, rev):
    c = GLA_CHUNK
    q = qk[:, :GLA_KW].astype(F32) * GLA_DK ** -0.5
    k = qk[:, GLA_KW:].astype(F32)
    b_total = b[0:1] if rev else b[c - 1:c]
    q_inter = (q * jnp.exp(b)).astype(BF16)
    k_state = (k * jnp.exp(b_total - b)).astype(BF16)
    decay_total = jnp.exp(b_total)

    row = lax.broadcasted_iota(jnp.int32, (c, GLA_KW), 0)
    pairs = []
    for size in GLA_LEVELS:
        n = c // size
        if size == GLA_LEVELS[0]:
            if rev:
                picks = [(i + 1) * size if i + 1 < n else None for i in range(n)]
            else:
                picks = [i * size - 1 if i > 0 else None for i in range(n)]
            ref = _stack_rows(b, picks, size)
            q_hat = q * jnp.exp(b - ref)
            k_hat = k * jnp.exp(jnp.minimum(ref - b, GLA_EXP_CLAMP))
        else:
            half = size // 2
            ref = _stack_rows(b, [i * size + half - (0 if rev else 1) for i in range(n)], size)
            late = (row & half) != 0
            q_side = jnp.logical_not(late) if rev else late
            q_hat = q * jnp.exp(jnp.where(q_side, b - ref, -jnp.inf))
            k_hat = k * jnp.exp(jnp.where(q_side, -jnp.inf, ref - b))
        pairs.append((q_hat.astype(BF16), k_hat.astype(BF16)))
    return q_inter, k_state, decay_total, pairs


def _gla_kernel(qkf_ref, vf_ref, lrf_ref, qkb_ref, vb_ref, lrb_ref, a_ref, bias_ref, lvl_ref, s0_ref,
                of_ref, ob_ref, so_ref, st_ref):
    j = pl.program_id(1)

    @pl.when(j == 0)
    def _():
        for z in range(2):
            for h in range(GLA_HEADS):
                st_ref[z, h] = s0_ref[0, z, h].T

    n_chunks = GLA_BLOCK // GLA_CHUNK
    chunk_rows = [slice(c * GLA_CHUNK, (c + 1) * GLA_CHUNK) for c in range(n_chunks)]
    units = [(z, rows) for rows_f, rows_b in zip(chunk_rows, reversed(chunk_rows))
             for z, rows in ((0, rows_f), (1, rows_b))]
    src = ((qkf_ref, vf_ref, lrf_ref, of_ref), (qkb_ref, vb_ref, lrb_ref, ob_ref))
    n_levels = len(GLA_LEVELS)
    hk = [slice(h * GLA_DK, (h + 1) * GLA_DK) for h in range(GLA_HEADS)]
    hv = [slice(h * GLA_DV, (h + 1) * GLA_DV) for h in range(GLA_HEADS)]
    tn = (((0,), (0,)), ((), ()))

    g = [_gla_decays(src[z][2][rows, :], a_ref[z], bias_ref[z]) for z, rows in units]
    b = [_gla_cumsum(gi, lvl_ref[z]) for gi, (z, rows) in zip(g, units)]
    ops = [_gla_operands(src[z][0][rows, :], bi, z == 1) for bi, (z, rows) in zip(b, units)]
    v = [src[z][1][rows, :] for z, rows in units]
    delta = [[lax.dot_general(vi[:, hv[h]], op[1][:, hk[h]], tn, preferred_element_type=F32)
              for h in range(GLA_HEADS)] for vi, op in zip(v, ops)]
    scores = [[[lax.dot_general(qh[:, hk[h]], kh[:, hk[h]], _NT, preferred_element_type=F32)
                for qh, kh in op[3]] for h in range(GLA_HEADS)] for op in ops]

    state = [[st_ref[z, h] for h in range(GLA_HEADS)] for z in range(2)]
    for u, (z, rows) in enumerate(units):
        lvl = lvl_ref[z]
        q_inter, _, decay_total, _ = ops[u]
        for h in range(GLA_HEADS):
            p = scores[u][h]
            att = p[-1]
            for i in reversed(range(n_levels - 1)):
                att = jnp.where(lvl == i, p[i], att)
            o = jnp.dot(att.astype(BF16), v[u][:, hv[h]], preferred_element_type=F32) + lax.dot_general(
                q_inter[:, hk[h]], state[z][h].astype(BF16), _NT, preferred_element_type=F32)
            src[z][3][rows, hv[h]] = o.astype(BF16)
            state[z][h] = state[z][h] * decay_total[:, hk[h]] + delta[u][h]
    for z in range(2):
        for h in range(GLA_HEADS):
            st_ref[z, h] = state[z][h]

    @pl.when(j == pl.num_programs(1) - 1)
    def _():
        for z in range(2):
            for h in range(GLA_HEADS):
                so_ref[0, z, h] = st_ref[z, h].T


def _gla(proj, lr, a_pad, bias, s0, batch, seq_len):
    t = proj.shape[0]
    nb = seq_len // GLA_BLOCK
    fwd = lambda b, j: b * nb + j
    bwd = lambda b, j: b * nb + nb - 1 - j
    qk_w = 2 * GLA_KW

    def proj_specs(row):
        return [pl.BlockSpec((GLA_BLOCK, qk_w), lambda b, j: (row(b, j), COL_GQK // qk_w)),
                pl.BlockSpec((GLA_BLOCK, GLA_VW), lambda b, j: (row(b, j), COL_GV // GLA_VW)),
                pl.BlockSpec((GLA_BLOCK, LR_BLOCK), lambda b, j: (row(b, j), 0))]

    state_spec = pl.BlockSpec((1, 2, GLA_HEADS, GLA_DK, GLA_DV), lambda b, j: (b, 0, 0, 0, 0))
    return pl.pallas_call(
        _gla_kernel,
        grid=(batch, nb),
        in_specs=proj_specs(fwd) + proj_specs(bwd) + [
            pl.BlockSpec((2, LR_BLOCK, GLA_KW), lambda b, j: (0, 0, 0)),
            pl.BlockSpec((2, 1, GLA_KW), lambda b, j: (0, 0, 0)),
            pl.BlockSpec((2, GLA_CHUNK, GLA_CHUNK), lambda b, j: (0, 0, 0)),
            state_spec],
        out_specs=[pl.BlockSpec((GLA_BLOCK, GLA_VW), lambda b, j: (fwd(b, j), 0)),
                   pl.BlockSpec((GLA_BLOCK, GLA_VW), lambda b, j: (bwd(b, j), 0)),
                   state_spec],
        out_shape=[jax.ShapeDtypeStruct((t, GLA_VW), BF16),
                   jax.ShapeDtypeStruct((t, GLA_VW), BF16),
                   jax.ShapeDtypeStruct((batch, 2, GLA_HEADS, GLA_DK, GLA_DV), F32)],
        scratch_shapes=[pltpu.VMEM((2, GLA_HEADS, GLA_DV, GLA_DK), F32)],
        compiler_params=pltpu.CompilerParams(dimension_semantics=("parallel", "arbitrary"),
                                             vmem_limit_bytes=VMEM_LIMIT),
        name="gla",
    )(proj, proj, lr, proj, proj, lr, a_pad, bias, _gla_level_table(), s0)


HALO = 16


def _out_kernel(*refs, tm, seq_len, final):
    (x_ref, a_ref, ch_ref, cb_ref, cc_ref, cg_ref, chp_ref, ccp_ref, chn_ref, ccn_ref,
     gg_ref, of_ref, ob_ref, cw_ref, gnw_ref, gate_ref, w_ref) = refs[:17]
    pos = 17
    if final:
        fnw_ref = refs[pos]
        pos += 1
    o_ref, mix_ref = refs[pos:pos + 2]

    mix_ref[:, :ATT_WIDTH] = a_ref[...]

    u = cc_ref[...].astype(F32) * ch_ref[...].astype(F32)
    u_before = ccp_ref[HALO - 1:HALO, :].astype(F32) * chp_ref[HALO - 1:HALO, :].astype(F32)
    u_after = ccn_ref[0:1, :].astype(F32) * chn_ref[0:1, :].astype(F32)
    row = lax.broadcasted_iota(jnp.int32, u.shape, 0)
    seq_pos = (pl.program_id(0) * tm + row) & (seq_len - 1)
    u_prev = jnp.where(row == 0, u_before, pltpu.roll(u, 1, axis=0))
    u_prev = jnp.where(seq_pos == 0, 0.0, u_prev)
    u_next = jnp.where(row == tm - 1, u_after, pltpu.roll(u, tm - 1, axis=0))
    u_next = jnp.where(seq_pos == seq_len - 1, 0.0, u_next)
    conv = u_prev * cw_ref[0:1, :] + u * cw_ref[1:2, :] + u_next * cw_ref[2:3, :]
    out_b = cb_ref[...].astype(F32) * conv * _silu(cg_ref[...].astype(F32))
    mix_ref[:, ATT_WIDTH:ATT_WIDTH + out_b.shape[1]] = out_b.astype(BF16)

    o = of_ref[...].astype(F32) + ob_ref[...].astype(F32)
    gg = gg_ref[...].astype(F32)
    base = mix_ref.shape[1] - GLA_VW
    for h in range(GLA_HEADS):
        cols = slice(h * GLA_DV, (h + 1) * GLA_DV)
        y = _rms(o[:, cols], gnw_ref[...]) * _silu(gg[:, cols])
        mix_ref[:, base + h * GLA_DV:base + (h + 1) * GLA_DV] = y.astype(BF16)

    r = jnp.dot(mix_ref[...], w_ref[0], preferred_element_type=F32)
    xn = x_ref[...] + gate_ref[0] * r
    if final:
        xn = _rms(xn, fnw_ref[...])
    o_ref[...] = xn


def _out_projection(x, att, proj, o_f, o_b, conv_w, gla_norm_w, mod3, w_out, layer, final_norm_w, mod_row,
                    seq_len):
    t, d = x.shape
    tm = 256
    cw = conv_w.shape[1]
    final = final_norm_w is not None
    per_halo = tm // HALO
    last_halo = t // HALO - 1

    def col(c, width):
        return lambda i: (i, c // width)

    prev = lambda c: (lambda i: (jnp.maximum(i * per_halo - 1, 0), c // cw))
    nxt = lambda c: (lambda i: (jnp.minimum((i + 1) * per_halo, last_halo), c // cw))
    in_specs = [pl.BlockSpec((tm, d), lambda i: (i, 0)),
                pl.BlockSpec((tm, ATT_WIDTH), lambda i: (i, 0)),
                pl.BlockSpec((tm, cw), col(COL_CH, cw)),
                pl.BlockSpec((tm, cw), col(COL_CB, cw)),
                pl.BlockSpec((tm, cw), col(COL_CC, cw)),
                pl.BlockSpec((tm, cw), col(COL_CG, cw)),
                pl.BlockSpec((HALO, cw), prev(COL_CH)),
                pl.BlockSpec((HALO, cw), prev(COL_CC)),
                pl.BlockSpec((HALO, cw), nxt(COL_CH)),
                pl.BlockSpec((HALO, cw), nxt(COL_CC)),
                pl.BlockSpec((tm, GLA_VW), col(COL_GG, GLA_VW)),
                pl.BlockSpec((tm, GLA_VW), lambda i: (i, 0)),
                pl.BlockSpec((tm, GLA_VW), lambda i: (i, 0)),
                pl.BlockSpec((3, cw), lambda i: (0, 0)),
                pl.BlockSpec((1, GLA_DV), lambda i: (0, 0)),
                pl.BlockSpec((1, 1, d), lambda i: (mod_row(i * tm), 0, 2)),
                pl.BlockSpec((1,) + w_out.shape[1:], lambda i: (layer, 0, 0))]
    args = [x, att, proj, proj, proj, proj, proj, proj, proj, proj, proj, o_f, o_b,
            conv_w, gla_norm_w.reshape(1, GLA_DV), mod3, w_out]
    if final:
        in_specs.append(pl.BlockSpec((1, d), lambda i: (0, 0)))
        args.append(final_norm_w.reshape(1, d))
    return pl.pallas_call(
        functools.partial(_out_kernel, tm=tm, seq_len=seq_len, final=final),
        grid=(t // tm,),
        in_specs=in_specs,
        out_specs=pl.BlockSpec((tm, d), lambda i: (i, 0)),
        out_shape=jax.ShapeDtypeStruct((t, d), F32),
        scratch_shapes=[pltpu.VMEM((tm, w_out.shape[1]), BF16)],
        compiler_params=pltpu.CompilerParams(dimension_semantics=("parallel",),
                                             vmem_limit_bytes=VMEM_LIMIT),
        name="out_projection",
    )(*args)


def _rope_tables(seq_len):
    pos = jnp.arange(seq_len)
    row = (pos // GRID_W).astype(F32)
    col = (pos % GRID_W).astype(F32)
    n_freq = HEAD_DIM // 4
    inv = ROPE_THETA ** (-jnp.arange(n_freq, dtype=F32) / n_freq)
    ang_r = row[:, None] * inv[None, :]
    ang_c = col[:, None] * inv[None, :]
    cos = jnp.concatenate([jnp.cos(ang_r), jnp.cos(ang_r), jnp.cos(ang_c), jnp.cos(ang_c)], axis=-1)
    sin = jnp.concatenate([-jnp.sin(ang_r), jnp.sin(ang_r), -jnp.sin(ang_c), jnp.sin(ang_c)], axis=-1)
    return cos, sin


@jax.jit
def _forward(x_prompt, x_sample, cache_k, cache_v, state_gla, c, c_ctx, w_mod, b_mod, norm_w, w_in,
             q_norm_w, k_norm_w, conv_w, gla_a_up, gla_a_bias, gla_norm_w, w_out, final_norm_w):
    batch, seq, d = x_prompt.shape
    dec_batch, dec_seq, _ = x_sample.shape
    depth = w_in.shape[0]
    past = cache_k.shape[2]

    cond = jnp.zeros((8, d), F32).at[0].set(c_ctx).at[1:1 + dec_batch].set(c)
    mod3 = _modulation(cond, w_mod, b_mod).reshape(depth * 8, 1, 3 * d)
    rope_tables = _rope_tables(dec_seq)

    h = x_prompt.reshape(batch * seq, d)
    z = x_sample.reshape(dec_batch * dec_seq, d)
    zero_state = jnp.zeros((batch, 2, GLA_HEADS, GLA_DK, GLA_DV), F32)
    w_in_t = jnp.swapaxes(w_in, 1, 2).astype(BF16)
    w_lr_t = jnp.pad(w_in_t[:, COL_LR:, :], ((0, 0), (0, LR_BLOCK - (IN_WIDTH - COL_LR)), (0, 0)))
    w_out_b = w_out.astype(BF16)
    new_k, new_v, new_state = [], [], []
    for l in range(depth):
        a_pad = jnp.zeros((2, LR_BLOCK, GLA_KW), F32)
        for zdir in range(2):
            a_pad = a_pad.at[zdir, zdir * GLA_RANK:(zdir + 1) * GLA_RANK].set(gla_a_up[l, zdir])
        a_pad = a_pad.astype(BF16)
        bias = gla_a_bias[l].reshape(2, 1, GLA_KW)
        final_w = final_norm_w if l == depth - 1 else None

        row_ctx = lambda tok, l=l: l * 8
        proj, lr = _in_projection(h, mod3, norm_w[l], w_in_t, w_lr_t, l, row_ctx)
        kv, kn_f32 = _kv_prepare(proj, k_norm_w[l], None, None, batch, seq, True)
        att = _attention(proj, kv, q_norm_w[l], None, batch, seq)
        o_f, o_b, s_out = _gla(proj, lr, a_pad, bias, zero_state, batch, seq)
        h = _out_projection(h, att, proj, o_f, o_b, conv_w[l], gla_norm_w[l], mod3, w_out_b, l, final_w,
                            row_ctx, seq)
        new_k.append(kn_f32.reshape(batch, seq, ATT_KV_HEADS, HEAD_DIM))
        new_v.append(proj[:, COL_V:COL_V + KV_WIDTH].astype(F32).reshape(batch, seq, ATT_KV_HEADS, HEAD_DIM))
        new_state.append(s_out)

        row_lat = lambda tok, l=l: l * 8 + 1 + tok // dec_seq
        proj, lr = _in_projection(z, mod3, norm_w[l], w_in_t, w_lr_t, l, row_lat)
        ctx_kv = (cache_k[:, l].reshape(dec_batch * past, KV_WIDTH),
                  cache_v[:, l].reshape(dec_batch * past, KV_WIDTH))
        (kv,) = _kv_prepare(proj, k_norm_w[l], rope_tables, ctx_kv, dec_batch, dec_seq, False)
        att = _attention(proj, kv, q_norm_w[l], rope_tables, dec_batch, dec_seq)
        o_f, o_b, _ = _gla(proj, lr, a_pad, bias, state_gla[:, l], dec_batch, dec_seq)
        z = _out_projection(z, att, proj, o_f, o_b, conv_w[l], gla_norm_w[l], mod3, w_out_b, l, final_w,
                            row_lat, dec_seq)

    return (h.reshape(batch, seq, d), z.reshape(dec_batch, dec_seq, d),
            jnp.stack(new_k, axis=1), jnp.stack(new_v, axis=1), jnp.stack(new_state, axis=1))


def kernel(x_prompt, x_sample, cache_k, cache_v, state_gla, c, c_ctx, w_mod, b_mod, norm_w, w_in, q_norm_w,
           k_norm_w, conv_w, gla_a_up, gla_a_bias, gla_norm_w, w_out, final_norm_w):
    return _forward(x_prompt, x_sample, cache_k, cache_v, state_gla, c, c_ctx, w_mod, b_mod, norm_w, w_in,
                    q_norm_w, k_norm_w, conv_w, gla_a_up, gla_a_bias, gla_norm_w, w_out, final_norm_w)
```

```python
import functools

import numpy as np
import jax
import jax.numpy as jnp
from jax import lax
from jax.experimental import pallas as pl
from jax.experimental.pallas import tpu as pltpu

F32 = jnp.float32
BF16 = jnp.bfloat16

GRID_W = 64
HEAD_DIM = 128
ATT_HEADS = 8
ATT_KV_HEADS = 2
HEADS_PER_KV = ATT_HEADS // ATT_KV_HEADS
ATT_WIDTH = ATT_HEADS * HEAD_DIM
KV_WIDTH = ATT_KV_HEADS * HEAD_DIM
GLA_HEADS = 4
GLA_DK = 64
GLA_DV = 128
GLA_KW = GLA_HEADS * GLA_DK
GLA_VW = GLA_HEADS * GLA_DV
GLA_RANK = 16
GLA_TAU = 16.0
ROPE_THETA = 10000.0
LOG2_E = 1.4426950408889634
ATT_DIRECT_EXP_LIMIT = 48.0
ATT_LOOKAHEAD = 2
EPS = 1e-6

COL_Q = 0
COL_K = 1024
COL_V = 1280
COL_AG = 1536
COL_CH = 2560
COL_CB = 3072
COL_CC = 3584
COL_CG = 4096
COL_GQK = 4608
COL_GV = 5120
COL_GG = 5632
COL_LR = 6144
IN_WIDTH = 6176
IN_TILE = 1024
IN_ROW_CHUNKS = 4
IN_MAIN = COL_LR
LR_BLOCK = 128

GLA_CHUNK = 128
GLA_LEVELS = (16, 32, 64, 128)
GLA_BLOCK = 256
GLA_EXP_CLAMP = 60.0

VMEM_LIMIT = 56 * 1024 * 1024


def _silu(x):
    return x * jax.nn.sigmoid(x)


def _rms(x, w):
    return x * lax.rsqrt(jnp.mean(x * x, axis=-1, keepdims=True) + EPS) * w


def _rope(y, cos, sin_signed):
    lane = lax.broadcasted_iota(jnp.int32, y.shape, 1)
    partner = jnp.where((lane & 63) < 32, pltpu.roll(y, 96, axis=1), pltpu.roll(y, 32, axis=1))
    return y * cos + partner * sin_signed


def _mod_kernel(c_ref, w_ref, b_ref, o_ref):
    a = _silu(c_ref[...]).astype(BF16)
    o_ref[0] = jnp.dot(a, w_ref[0].astype(BF16), preferred_element_type=F32) + b_ref[0]


def _modulation(cond, w_mod, b_mod):
    depth, d, n = w_mod.shape
    tn = 768
    return pl.pallas_call(
        _mod_kernel,
        grid=(depth, n // tn),
        in_specs=[pl.BlockSpec((8, d), lambda l, j: (0, 0)),
                  pl.BlockSpec((1, d, tn), lambda l, j: (l, 0, j)),
                  pl.BlockSpec((1, 1, tn), lambda l, j: (l, 0, j))],
        out_specs=pl.BlockSpec((1, 8, tn), lambda l, j: (l, 0, j)),
        out_shape=jax.ShapeDtypeStruct((depth, 8, n), F32),
        compiler_params=pltpu.CompilerParams(dimension_semantics=("parallel", "parallel"),
                                             vmem_limit_bytes=VMEM_LIMIT),
        name="modulation",
    )(cond, w_mod, b_mod.reshape(depth, 1, n))


_NT = (((1,), (1,)), ((), ()))


def _in_kernel(x_ref, nw_ref, shift_ref, scale_ref, w_ref, wlr_ref, o_ref, lr_ref, xn_ref):
    def project(xn, w):
        return lax.dot_general(xn, w, _NT, preferred_element_type=F32).astype(BF16)

    @pl.when(pl.program_id(1) == 0)
    def _():
        gain = nw_ref[...] * (1.0 + scale_ref[0])
        per = x_ref.shape[0] // IN_ROW_CHUNKS
        for r in range(IN_ROW_CHUNKS):
            rows = slice(r * per, (r + 1) * per)
            x = x_ref[rows, :]
            inv = lax.rsqrt(jnp.mean(x * x, axis=-1, keepdims=True) + EPS)
            xn = (x * inv * gain + shift_ref[0]).astype(BF16)
            xn_ref[rows, :] = xn
            o_ref[rows, :] = project(xn, w_ref[0])
            lr_ref[rows, :] = project(xn, wlr_ref[0])

    @pl.when(pl.program_id(1) > 0)
    def _():
        o_ref[...] = project(xn_ref[...], w_ref[0])


def _in_projection(x, mod3, norm_w, w_in_t, w_lr_t, layer, mod_row):
    t, d = x.shape
    tm = 1024
    return pl.pallas_call(
        _in_kernel,
        grid=(t // tm, IN_MAIN // IN_TILE),
        in_specs=[pl.BlockSpec((tm, d), lambda i, j: (i, 0)),
                  pl.BlockSpec((1, d), lambda i, j: (0, 0)),
                  pl.BlockSpec((1, 1, d), lambda i, j: (mod_row(i * tm), 0, 0)),
                  pl.BlockSpec((1, 1, d), lambda i, j: (mod_row(i * tm), 0, 1)),
                  pl.BlockSpec((1, IN_TILE, d), lambda i, j: (layer, j, 0)),
                  pl.BlockSpec((1, LR_BLOCK, d), lambda i, j: (layer, 0, 0))],
        out_specs=[pl.BlockSpec((tm, IN_TILE), lambda i, j: (i, j)),
                   pl.BlockSpec((tm, LR_BLOCK), lambda i, j: (i, 0))],
        out_shape=[jax.ShapeDtypeStruct((t, IN_MAIN), BF16),
                   jax.ShapeDtypeStruct((t, LR_BLOCK), BF16)],
        scratch_shapes=[pltpu.VMEM((tm, d), BF16)],
        compiler_params=pltpu.CompilerParams(dimension_semantics=("parallel", "arbitrary"),
                                             vmem_limit_bytes=VMEM_LIMIT),
        name="in_projection",
    )(x, norm_w.reshape(1, d), mod3, mod3, w_in_t, w_lr_t)


def _kv_kernel(*refs, n_self, rope, has_ctx, emit_f32):
    kv_ref, kw_ref = refs[:2]
    pos = 2
    if rope:
        cos_ref, sin_ref = refs[pos:pos + 2]
        pos += 2
    if has_ctx:
        ck_ref, cv_ref = refs[pos:pos + 2]
        pos += 2
    k_ref, vt_ref, kmax_ref = refs[pos:pos + 3]

    @pl.when(pl.program_id(1) == 0)
    def _():
        kmax_ref[...] = jnp.zeros(kmax_ref.shape, F32)

    def put_keys(g, kb):
        cols = slice(g * HEAD_DIM, (g + 1) * HEAD_DIM)
        k_ref[:, cols] = kb
        kf = kb.astype(F32)
        n2 = jnp.max(jnp.sum(kf * kf, axis=1, keepdims=True), axis=0, keepdims=True)
        kmax_ref[0, g:g + 1, :] = jnp.maximum(kmax_ref[0, g:g + 1, :], jnp.broadcast_to(n2, (1, HEAD_DIM)))

    def own_rows():
        for g in range(ATT_KV_HEADS):
            cols = slice(g * HEAD_DIM, (g + 1) * HEAD_DIM)
            y = _rms(kv_ref[:, cols].astype(F32), kw_ref[...])
            if rope:
                y = _rope(y, cos_ref[...], sin_ref[...])
            put_keys(g, y.astype(BF16))
            if emit_f32:
                refs[pos + 3][:, cols] = y
        vt_ref[...] = kv_ref[:, KV_WIDTH:].astype(F32).T.astype(BF16)

    if has_ctx:
        pl.when(pl.program_id(1) < n_self)(own_rows)

        @pl.when(pl.program_id(1) >= n_self)
        def _():
            for g in range(ATT_KV_HEADS):
                put_keys(g, ck_ref[:, g * HEAD_DIM:(g + 1) * HEAD_DIM].astype(BF16))
            vt_ref[...] = cv_ref[...].T.astype(BF16)
    else:
        own_rows()


def _kv_prepare(proj, k_norm_w, rope_tables, ctx_kv, batch, seq_len, emit_f32):
    tr = min(512, seq_len)
    n_self = seq_len // tr
    rope = rope_tables is not None
    has_ctx = ctx_kv is not None
    assert not (has_ctx and emit_f32)
    n_ctx = ctx_kv[0].shape[0] // batch // tr if has_ctx else 0
    n_all = n_self + n_ctx
    own = lambda j: jnp.minimum(j, n_self - 1)
    in_specs = [pl.BlockSpec((tr, 2 * KV_WIDTH), lambda b, j: (b * n_self + own(j), COL_K // (2 * KV_WIDTH))),
                pl.BlockSpec((1, HEAD_DIM), lambda b, j: (0, 0))]
    args = [proj, k_norm_w.reshape(1, HEAD_DIM)]
    if rope:
        in_specs += [pl.BlockSpec((tr, HEAD_DIM), lambda b, j: (own(j), 0))] * 2
        args += list(rope_tables)
    if has_ctx:
        in_specs += [pl.BlockSpec((tr, KV_WIDTH), lambda b, j: (b * n_ctx + jnp.maximum(j - n_self, 0), 0))] * 2
        args += list(ctx_kv)
    out_specs = [pl.BlockSpec((tr, KV_WIDTH), lambda b, j: (b * n_all + j, 0)),
                 pl.BlockSpec((KV_WIDTH, tr), lambda b, j: (b, j)),
                 pl.BlockSpec((1, 8, HEAD_DIM), lambda b, j: (b, 0, 0))]
    out_shape = [jax.ShapeDtypeStruct((batch * n_all * tr, KV_WIDTH), BF16),
                 jax.ShapeDtypeStruct((batch * KV_WIDTH, n_all * tr), BF16),
                 jax.ShapeDtypeStruct((batch, 8, HEAD_DIM), F32)]
    if emit_f32:
        out_specs.append(pl.BlockSpec((tr, KV_WIDTH), lambda b, j: (b * n_self + j, 0)))
        out_shape.append(jax.ShapeDtypeStruct((batch * seq_len, KV_WIDTH), F32))
    return pl.pallas_call(
        functools.partial(_kv_kernel, n_self=n_self, rope=rope, has_ctx=has_ctx, emit_f32=emit_f32),
        grid=(batch, n_all),
        in_specs=in_specs,
        out_specs=out_specs,
        out_shape=out_shape,
        compiler_params=pltpu.CompilerParams(dimension_semantics=("parallel", "arbitrary")),
        name="kv_prepare",
    )(*args)


def _attn_kernel(*refs, tq, tk, n_chunks, rope):
    q_ref, gate_ref, k_ref, vt_ref, kmax_ref, qw_ref = refs[:6]
    pos = 6
    if rope:
        cos_ref, sin_ref = refs[pos:pos + 2]
        pos += 2
    o_ref, qt_ref, s0_ref, s1_ref, m_ref, l_ref, acc_ref = refs[pos:pos + 7]

    q_scale = HEAD_DIM ** -0.5 * LOG2_E
    heads = [slice(h * tq, (h + 1) * tq) for h in range(HEADS_PER_KV)]
    dims = [slice(h * HEAD_DIM, (h + 1) * HEAD_DIM) for h in range(HEADS_PER_KV)]

    def rows(c):
        off = c * tk
        return pl.ds(off if isinstance(off, int) else pl.multiple_of(off, tk), tk)

    def prepare(h):
        y = _rms(q_ref[:, dims[h]].astype(F32), qw_ref[...])
        if rope:
            y = _rope(y, cos_ref[...], sin_ref[...])
        qt_ref[:, heads[h]] = (y * q_scale).T.astype(BF16)
        l_ref[:, heads[h]] = jnp.zeros((1, tq), F32)
        acc_ref[:, heads[h]] = jnp.zeros((HEAD_DIM, tq), F32)

    def finish(h):
        o = (acc_ref[:, heads[h]] / l_ref[:, heads[h]]).T
        o_ref[:, dims[h]] = (o * _silu(gate_ref[:, dims[h]].astype(F32))).astype(BF16)

    w = qw_ref[...]
    q_norm2 = HEAD_DIM * q_scale * q_scale * jnp.max(w * w)
    key_norm2 = jnp.max(kmax_ref[0, pl.ds(pl.program_id(1), 1), :])
    bounded = q_norm2 * key_norm2 < ATT_DIRECT_EXP_LIMIT ** 2

    @pl.when(bounded)
    def _():
        pieces = [(h, c) for h in range(HEADS_PER_KV) for c in range(n_chunks)]
        pending = {}
        for idx in range(len(pieces) + ATT_LOOKAHEAD):
            if idx < len(pieces):
                h, c = pieces[idx]
                if c == 0:
                    prepare(h)
                pending[idx] = jnp.dot(k_ref[rows(c), :], qt_ref[:, heads[h]], preferred_element_type=F32)
            if idx >= ATT_LOOKAHEAD:
                h, c = pieces[idx - ATT_LOOKAHEAD]
                p = jnp.exp2(pending.pop(idx - ATT_LOOKAHEAD))
                l_ref[:, heads[h]] += jnp.sum(p, axis=0, keepdims=True)
                acc_ref[:, heads[h]] += jnp.dot(vt_ref[:, rows(c)], p.astype(BF16), preferred_element_type=F32)
                if c == n_chunks - 1:
                    finish(h)

    @pl.when(jnp.logical_not(bounded))
    def _():
        for h in range(HEADS_PER_KV):
            prepare(h)
        m_ref[...] = jnp.full(m_ref.shape, -jnp.inf, F32)

        def scores(c, s_ref):
            s_ref[...] = jnp.dot(k_ref[rows(c), :], qt_ref[...], preferred_element_type=F32)

        def update(c, s_ref):
            s = s_ref[...]
            m_old = m_ref[...]
            m_new = jnp.maximum(m_old, jnp.max(s, axis=0, keepdims=True))
            p = jnp.exp2(s - m_new)
            alpha = jnp.exp2(m_old - m_new)
            l_ref[...] = alpha * l_ref[...] + jnp.sum(p, axis=0, keepdims=True)
            pv = jnp.dot(vt_ref[:, rows(c)], p.astype(BF16), preferred_element_type=F32)
            acc_ref[...] = alpha * acc_ref[...] + pv
            m_ref[...] = m_new

        scores(0, s0_ref)
        n_pairs = (n_chunks - 1) // 2

        def body(i, carry):
            c = 2 * i
            scores(c + 1, s1_ref)
            update(c, s0_ref)
            scores(c + 2, s0_ref)
            update(c + 1, s1_ref)
            return carry

        lax.fori_loop(0, n_pairs, body, 0)
        done = 2 * n_pairs
        if n_chunks - done == 2:
            scores(done + 1, s1_ref)
            update(done, s0_ref)
            update(done + 1, s1_ref)
        else:
            update(done, s0_ref)
        for h in range(HEADS_PER_KV):
            finish(h)


def _attention(proj, k_all, vt_all, k_norm2, q_norm_w, rope_tables, batch, seq_len):
    t = proj.shape[0]
    keys = k_all.shape[0] // batch
    tq = 256
    tk = min(512, keys)
    nq = seq_len // tq
    gw = HEADS_PER_KV * HEAD_DIM
    rope = rope_tables is not None
    in_specs = [pl.BlockSpec((tq, gw), lambda b, g, i: (b * nq + i, COL_Q // gw + g)),
                pl.BlockSpec((tq, gw), lambda b, g, i: (b * nq + i, COL_AG // gw + g)),
                pl.BlockSpec((keys, HEAD_DIM), lambda b, g, i: (b, g)),
                pl.BlockSpec((HEAD_DIM, keys), lambda b, g, i: (b * ATT_KV_HEADS + g, 0)),
                pl.BlockSpec((1,) + k_norm2.shape[1:], lambda b, g, i: (b, 0, 0)),
                pl.BlockSpec((1, HEAD_DIM), lambda b, g, i: (0, 0))]
    args = [proj, proj, k_all, vt_all, k_norm2, q_norm_w.reshape(1, HEAD_DIM)]
    if rope:
        in_specs += [pl.BlockSpec((tq, HEAD_DIM), lambda b, g, i: (i, 0))] * 2
        args += list(rope_tables)
    return pl.pallas_call(
        functools.partial(_attn_kernel, tq=tq, tk=tk, n_chunks=keys // tk, rope=rope),
        grid=(batch, ATT_KV_HEADS, nq),
        in_specs=in_specs,
        out_specs=pl.BlockSpec((tq, gw), lambda b, g, i: (b * nq + i, g)),
        out_shape=jax.ShapeDtypeStruct((t, ATT_WIDTH), BF16),
        scratch_shapes=[pltpu.VMEM((HEAD_DIM, HEADS_PER_KV * tq), BF16),
                        pltpu.VMEM((tk, HEADS_PER_KV * tq), F32),
                        pltpu.VMEM((tk, HEADS_PER_KV * tq), F32),
                        pltpu.VMEM((1, HEADS_PER_KV * tq), F32),
                        pltpu.VMEM((1, HEADS_PER_KV * tq), F32),
                        pltpu.VMEM((HEAD_DIM, HEADS_PER_KV * tq), F32)],
        compiler_params=pltpu.CompilerParams(dimension_semantics=("parallel", "parallel", "arbitrary"),
                                             vmem_limit_bytes=VMEM_LIMIT),
        name="attention",
    )(*args)


def _log_sigmoid(x):
    return -(jnp.maximum(-x, 0.0) + jnp.log(1.0 + jnp.exp(-jnp.abs(x))))


def _gla_level_table():
    t = np.arange(GLA_CHUNK)[:, None]
    s = np.arange(GLA_CHUNK)[None, :]
    lvl = np.full((GLA_CHUNK, GLA_CHUNK), len(GLA_LEVELS), np.int32)
    for i, size in reversed(list(enumerate(GLA_LEVELS))):
        lvl = np.where(t // size == s // size, i, lvl)
    fwd = np.where(s <= t, lvl, len(GLA_LEVELS)).astype(np.int32)
    return jnp.asarray(np.stack([fwd, fwd.T]))


def _stack_rows(b, picks, size):
    width = b.shape[1]
    return jnp.concatenate(
        [jnp.zeros((size, width), F32) if p is None else jnp.broadcast_to(b[p:p + 1], (size, width))
         for p in picks], axis=0)


def _gla_decays(lr, a, bias):
    logits = jnp.dot(lr, a, preferred_element_type=F32) + bias
    return _log_sigmoid(logits) * (1.0 / GLA_TAU)


def _gla_cumsum(g, lvl):
    g_hi = g.astype(BF16)
    g_lo = (g - g_hi.astype(F32)).astype(BF16)
    tri = jnp.where(lvl < len(GLA_LEVELS), 1.0, 0.0).astype(BF16)
    return jnp.dot(tri, g_hi, preferred_element_type=F32) + jnp.dot(tri, g_lo, preferred_element_type=F32)


def _gla_operands(qk, b, rev):
    c = GLA_CHUNK
    q = qk[:, :GLA_KW].astype(F32) * GLA_DK ** -0.5
    k = qk[:, GLA_KW:].astype(F32)
    b_total = b[0:1] if rev else b[c - 1:c]
    q_inter = (q * jnp.exp(b)).astype(BF16)
    k_state = (k * jnp.exp(b_total - b)).astype(BF16)
    decay_total = jnp.exp(b_total)

    row = lax.broadcasted_iota(jnp.int32, (c, GLA_KW), 0)
    pairs = []
    for size in GLA_LEVELS:
        n = c // size
        if size == GLA_LEVELS[0]:
            if rev:
                picks = [(i + 1) * size if i + 1 < n else None for i in range(n)]
            else:
                picks = [i * size - 1 if i > 0 else None for i in range(n)]
            ref = _stack_rows(b, picks, size)
            q_hat = q * jnp.exp(b - ref)
            k_hat = k * jnp.exp(jnp.minimum(ref - b, GLA_EXP_CLAMP))
        else:
            half = size // 2
            ref = _stack_rows(b, [i * size + half - (0 if rev else 1) for i in range(n)], size)
            late = (row & half) != 0
            q_side = jnp.logical_not(late) if rev else late
            q_hat = q * jnp.exp(jnp.where(q_side, b - ref, -jnp.inf))
            k_hat = k * jnp.exp(jnp.where(q_side, -jnp.inf, ref - b))
        pairs.append((q_hat.astype(BF16), k_hat.astype(BF16)))
    return q_inter, k_state, decay_total, pairs


def _gla_kernel(qkf_ref, vf_ref, lrf_ref, qkb_ref, vb_ref, lrb_ref, a_ref, bias_ref, lvl_ref, s0_ref,
                of_ref, ob_ref, so_ref, st_ref):
    j = pl.program_id(1)

    @pl.when(j == 0)
    def _():
        for z in range(2):
            for h in range(GLA_HEADS):
                st_ref[z, h] = s0_ref[0, z, h].T

    n_chunks = GLA_BLOCK // GLA_CHUNK
    chunk_rows = [slice(c * GLA_CHUNK, (c + 1) * GLA_CHUNK) for c in range(n_chunks)]
    units = [(z, rows) for rows_f, rows_b in zip(chunk_rows, reversed(chunk_rows))
             for z, rows in ((0, rows_f), (1, rows_b))]
    src = ((qkf_ref, vf_ref, lrf_ref, of_ref), (qkb_ref, vb_ref, lrb_ref, ob_ref))
    n_levels = len(GLA_LEVELS)
    hk = [slice(h * GLA_DK, (h + 1) * GLA_DK) for h in range(GLA_HEADS)]
    hv = [slice(h * GLA_DV, (h + 1) * GLA_DV) for h in range(GLA_HEADS)]
    tn = (((0,), (0,)), ((), ()))

    g = [_gla_decays(src[z][2][rows, :], a_ref[z], bias_ref[z]) for z, rows in units]
    b = [_gla_cumsum(gi, lvl_ref[z]) for gi, (z, rows) in zip(g, units)]
    ops = [_gla_operands(src[z][0][rows, :], bi, z == 1) for bi, (z, rows) in zip(b, units)]
    v = [src[z][1][rows, :] for z, rows in units]
    delta = [[lax.dot_general(vi[:, hv[h]], op[1][:, hk[h]], tn, preferred_element_type=F32)
              for h in range(GLA_HEADS)] for vi, op in zip(v, ops)]
    scores = [[[lax.dot_general(qh[:, hk[h]], kh[:, hk[h]], _NT, preferred_element_type=F32)
                for qh, kh in op[3]] for h in range(GLA_HEADS)] for op in ops]

    state = [[st_ref[z, h] for h in range(GLA_HEADS)] for z in range(2)]
    for u, (z, rows) in enumerate(units):
        lvl = lvl_ref[z]
        q_inter, _, decay_total, _ = ops[u]
        for h in range(GLA_HEADS):
            p = scores[u][h]
            att = p[-1]
            for i in reversed(range(n_levels - 1)):
                att = jnp.where(lvl == i, p[i], att)
            o = jnp.dot(att.astype(BF16), v[u][:, hv[h]], preferred_element_type=F32) + lax.dot_general(
                q_inter[:, hk[h]], state[z][h].astype(BF16), _NT, preferred_element_type=F32)
            src[z][3][rows, hv[h]] = o.astype(BF16)
            state[z][h] = state[z][h] * decay_total[:, hk[h]] + delta[u][h]
    for z in range(2):
        for h in range(GLA_HEADS):
            st_ref[z, h] = state[z][h]

    @pl.when(j == pl.num_programs(1) - 1)
    def _():
        for z in range(2):
            for h in range(GLA_HEADS):
                so_ref[0, z, h] = st_ref[z, h].T


def _gla(proj, lr, a_pad, bias, s0, batch, seq_len):
    t = proj.shape[0]
    nb = seq_len // GLA_BLOCK
    fwd = lambda b, j: b * nb + j
    bwd = lambda b, j: b * nb + nb - 1 - j
    qk_w = 2 * GLA_KW

    def proj_specs(row):
        return [pl.BlockSpec((GLA_BLOCK, qk_w), lambda b, j: (row(b, j), COL_GQK // qk_w)),
                pl.BlockSpec((GLA_BLOCK, GLA_VW), lambda b, j: (row(b, j), COL_GV // GLA_VW)),
                pl.BlockSpec((GLA_BLOCK, LR_BLOCK), lambda b, j: (row(b, j), 0))]

    state_spec = pl.BlockSpec((1, 2, GLA_HEADS, GLA_DK, GLA_DV), lambda b, j: (b, 0, 0, 0, 0))
    return pl.pallas_call(
        _gla_kernel,
        grid=(batch, nb),
        in_specs=proj_specs(fwd) + proj_specs(bwd) + [
            pl.BlockSpec((2, LR_BLOCK, GLA_KW), lambda b, j: (0, 0, 0)),
            pl.BlockSpec((2, 1, GLA_KW), lambda b, j: (0, 0, 0)),
            pl.BlockSpec((2, GLA_CHUNK, GLA_CHUNK), lambda b, j: (0, 0, 0)),
            state_spec],
        out_specs=[pl.BlockSpec((GLA_BLOCK, GLA_VW), lambda b, j: (fwd(b, j), 0)),
                   pl.BlockSpec((GLA_BLOCK, GLA_VW), lambda b, j: (bwd(b, j), 0)),
                   state_spec],
        out_shape=[jax.ShapeDtypeStruct((t, GLA_VW), BF16),
                   jax.ShapeDtypeStruct((t, GLA_VW), BF16),
                   jax.ShapeDtypeStruct((batch, 2, GLA_HEADS, GLA_DK, GLA_DV), F32)],
        scratch_shapes=[pltpu.VMEM((2, GLA_HEADS, GLA_DV, GLA_DK), F32)],
        compiler_params=pltpu.CompilerParams(dimension_semantics=("parallel", "arbitrary"),
                                             vmem_limit_bytes=VMEM_LIMIT),
        name="gla",
    )(proj, proj, lr, proj, proj, lr, a_pad, bias, _gla_level_table(), s0)


HALO = 16


def _out_kernel(*refs, tm, seq_len, final):
    (x_ref, a_ref, ch_ref, cb_ref, cc_ref, cg_ref, chp_ref, ccp_ref, chn_ref, ccn_ref,
     gg_ref, of_ref, ob_ref, cw_ref, gnw_ref, gate_ref, w_ref) = refs[:17]
    pos = 17
    if final:
        fnw_ref = refs[pos]
        pos += 1
    o_ref, mix_ref = refs[pos:pos + 2]

    mix_ref[:, :ATT_WIDTH] = a_ref[...]

    u = cc_ref[...].astype(F32) * ch_ref[...].astype(F32)
    u_before = ccp_ref[HALO - 1:HALO, :].astype(F32) * chp_ref[HALO - 1:HALO, :].astype(F32)
    u_after = ccn_ref[0:1, :].astype(F32) * chn_ref[0:1, :].astype(F32)
    row = lax.broadcasted_iota(jnp.int32, u.shape, 0)
    seq_pos = (pl.program_id(0) * tm + row) & (seq_len - 1)
    u_prev = jnp.where(row == 0, u_before, pltpu.roll(u, 1, axis=0))
    u_prev = jnp.where(seq_pos == 0, 0.0, u_prev)
    u_next = jnp.where(row == tm - 1, u_after, pltpu.roll(u, tm - 1, axis=0))
    u_next = jnp.where(seq_pos == seq_len - 1, 0.0, u_next)
    conv = u_prev * cw_ref[0:1, :] + u * cw_ref[1:2, :] + u_next * cw_ref[2:3, :]
    out_b = cb_ref[...].astype(F32) * conv * _silu(cg_ref[...].astype(F32))
    mix_ref[:, ATT_WIDTH:ATT_WIDTH + out_b.shape[1]] = out_b.astype(BF16)

    o = of_ref[...].astype(F32) + ob_ref[...].astype(F32)
    gg = gg_ref[...].astype(F32)
    base = mix_ref.shape[1] - GLA_VW
    for h in range(GLA_HEADS):
        cols = slice(h * GLA_DV, (h + 1) * GLA_DV)
        y = _rms(o[:, cols], gnw_ref[...]) * _silu(gg[:, cols])
        mix_ref[:, base + h * GLA_DV:base + (h + 1) * GLA_DV] = y.astype(BF16)

    r = jnp.dot(mix_ref[...], w_ref[0], preferred_element_type=F32)
    xn = x_ref[...] + gate_ref[0] * r
    if final:
        xn = _rms(xn, fnw_ref[...])
    o_ref[...] = xn


def _out_projection(x, att, proj, o_f, o_b, conv_w, gla_norm_w, mod3, w_out, layer, final_norm_w, mod_row,
                    seq_len):
    t, d = x.shape
    tm = 256
    cw = conv_w.shape[1]
    final = final_norm_w is not None
    per_halo = tm // HALO
    last_halo = t // HALO - 1

    def col(c, width):
        return lambda i: (i, c // width)

    prev = lambda c: (lambda i: (jnp.maximum(i * per_halo - 1, 0), c // cw))
    nxt = lambda c: (lambda i: (jnp.minimum((i + 1) * per_halo, last_halo), c // cw))
    in_specs = [pl.BlockSpec((tm, d), lambda i: (i, 0)),
                pl.BlockSpec((tm, ATT_WIDTH), lambda i: (i, 0)),
                pl.BlockSpec((tm, cw), col(COL_CH, cw)),
                pl.BlockSpec((tm, cw), col(COL_CB, cw)),
                pl.BlockSpec((tm, cw), col(COL_CC, cw)),
                pl.BlockSpec((tm, cw), col(COL_CG, cw)),
                pl.BlockSpec((HALO, cw), prev(COL_CH)),
                pl.BlockSpec((HALO, cw), prev(COL_CC)),
                pl.BlockSpec((HALO, cw), nxt(COL_CH)),
                pl.BlockSpec((HALO, cw), nxt(COL_CC)),
                pl.BlockSpec((tm, GLA_VW), col(COL_GG, GLA_VW)),
                pl.BlockSpec((tm, GLA_VW), lambda i: (i, 0)),
                pl.BlockSpec((tm, GLA_VW), lambda i: (i, 0)),
                pl.BlockSpec((3, cw), lambda i: (0, 0)),
                pl.BlockSpec((1, GLA_DV), lambda i: (0, 0)),
                pl.BlockSpec((1, 1, d), lambda i: (mod_row(i * tm), 0, 2)),
                pl.BlockSpec((1,) + w_out.shape[1:], lambda i: (layer, 0, 0))]
    args = [x, att, proj, proj, proj, proj, proj, proj, proj, proj, proj, o_f, o_b,
            conv_w, gla_norm_w.reshape(1, GLA_DV), mod3, w_out]
    if final:
        in_specs.append(pl.BlockSpec((1, d), lambda i: (0, 0)))
        args.append(final_norm_w.reshape(1, d))
    return pl.pallas_call(
        functools.partial(_out_kernel, tm=tm, seq_len=seq_len, final=final),
        grid=(t // tm,),
        in_specs=in_specs,
        out_specs=pl.BlockSpec((tm, d), lambda i: (i, 0)),
        out_shape=jax.ShapeDtypeStruct((t, d), F32),
        scratch_shapes=[pltpu.VMEM((tm, w_out.shape[1]), BF16)],
        compiler_params=pltpu.CompilerParams(dimension_semantics=("parallel",),
                                             vmem_limit_bytes=VMEM_LIMIT),
        name="out_projection",
    )(*args)


def _rope_tables(seq_len):
    pos = jnp.arange(seq_len)
    row = (pos // GRID_W).astype(F32)
    col = (pos % GRID_W).astype(F32)
    n_freq = HEAD_DIM // 4
    inv = ROPE_THETA ** (-jnp.arange(n_freq, dtype=F32) / n_freq)
    ang_r = row[:, None] * inv[None, :]
    ang_c = col[:, None] * inv[None, :]
    cos = jnp.concatenate([jnp.cos(ang_r), jnp.cos(ang_r), jnp.cos(ang_c), jnp.cos(ang_c)], axis=-1)
    sin = jnp.concatenate([-jnp.sin(ang_r), jnp.sin(ang_r), -jnp.sin(ang_c), jnp.sin(ang_c)], axis=-1)
    return cos, sin


@jax.jit
def _forward(x_prompt, x_sample, cache_k, cache_v, state_gla, c, c_ctx, w_mod, b_mod, norm_w, w_in,
             q_norm_w, k_norm_w, conv_w, gla_a_up, gla_a_bias, gla_norm_w, w_out, final_norm_w):
    batch, seq, d = x_prompt.shape
    dec_batch, dec_seq, _ = x_sample.shape
    depth = w_in.shape[0]
    past = cache_k.shape[2]

    cond = jnp.zeros((8, d), F32).at[0].set(c_ctx).at[1:1 + dec_batch].set(c)
    mod3 = _modulation(cond, w_mod, b_mod).reshape(depth * 8, 1, 3 * d)
    rope_tables = _rope_tables(dec_seq)

    h = x_prompt.reshape(batch * seq, d)
    z = x_sample.reshape(dec_batch * dec_seq, d)
    zero_state = jnp.zeros((batch, 2, GLA_HEADS, GLA_DK, GLA_DV), F32)
    w_in_t = jnp.swapaxes(w_in, 1, 2).astype(BF16)
    w_lr_t = jnp.pad(w_in_t[:, COL_LR:, :], ((0, 0), (0, LR_BLOCK - (IN_WIDTH - COL_LR)), (0, 0)))
    w_out_b = w_out.astype(BF16)
    new_k, new_v, new_state = [], [], []
    for l in range(depth):
        a_pad = jnp.zeros((2, LR_BLOCK, GLA_KW), F32)
        for zdir in range(2):
            a_pad = a_pad.at[zdir, zdir * GLA_RANK:(zdir + 1) * GLA_RANK].set(gla_a_up[l, zdir])
        a_pad = a_pad.astype(BF16)
        bias = gla_a_bias[l].reshape(2, 1, GLA_KW)
        final_w = final_norm_w if l == depth - 1 else None

        row_ctx = lambda tok, l=l: l * 8
        proj, lr = _in_projection(h, mod3, norm_w[l], w_in_t, w_lr_t, l, row_ctx)
        k_all, vt_all, k_norm2, kn_f32 = _kv_prepare(proj, k_norm_w[l], None, None, batch, seq, True)
        att = _attention(proj, k_all, vt_all, k_norm2, q_norm_w[l], None, batch, seq)
        o_f, o_b, s_out = _gla(proj, lr, a_pad, bias, zero_state, batch, seq)
        h = _out_projection(h, att, proj, o_f, o_b, conv_w[l], gla_norm_w[l], mod3, w_out_b, l, final_w,
                            row_ctx, seq)
        new_k.append(kn_f32.reshape(batch, seq, ATT_KV_HEADS, HEAD_DIM))
        new_v.append(proj[:, COL_V:COL_V + KV_WIDTH].astype(F32).reshape(batch, seq, ATT_KV_HEADS, HEAD_DIM))
        new_state.append(s_out)

        row_lat = lambda tok, l=l: l * 8 + 1 + tok // dec_seq
        proj, lr = _in_projection(z, mod3, norm_w[l], w_in_t, w_lr_t, l, row_lat)
        ctx_kv = (cache_k[:, l].reshape(dec_batch * past, KV_WIDTH),
                  cache_v[:, l].reshape(dec_batch * past, KV_WIDTH))
        k_all, vt_all, k_norm2 = _kv_prepare(proj, k_norm_w[l], rope_tables, ctx_kv, dec_batch, dec_seq, False)
        att = _attention(proj, k_all, vt_all, k_norm2, q_norm_w[l], rope_tables, dec_batch, dec_seq)
        o_f, o_b, _ = _gla(proj, lr, a_pad, bias, state_gla[:, l], dec_batch, dec_seq)
        z = _out_projection(z, att, proj, o_f, o_b, conv_w[l], gla_norm_w[l], mod3, w_out_b, l, final_w,
                            row_lat, dec_seq)

    return (h.reshape(batch, seq, d), z.reshape(dec_batch, dec_seq, d),
            jnp.stack(new_k, axis=1), jnp.stack(new_v, axis=1), jnp.stack(new_state, axis=1))


def kernel(x_prompt, x_sample, cache_k, cache_v, state_gla, c, c_ctx, w_mod, b_mod, norm_w, w_in, q_norm_w,
           k_norm_w, conv_w, gla_a_up, gla_a_bias, gla_norm_w, w_out, final_norm_w):
    return _forward(x_prompt, x_sample, cache_k, cache_v, state_gla, c, c_ctx, w_mod, b_mod, norm_w, w_in,
                    q_norm_w, k_norm_w, conv_w, gla_a_up, gla_a_bias, gla_norm_w, w_out, final_norm_w)
```

```python
import functools

import numpy as np
import jax
import jax.numpy as jnp
from jax import lax
from jax.experimental import pallas as pl
from jax.experimental.pallas import tpu as pltpu

F32 = jnp.float32
BF16 = jnp.bfloat16

GRID_W = 64
HEAD_DIM = 128
ATT_HEADS = 8
ATT_KV_HEADS = 2
HEADS_PER_KV = ATT_HEADS // ATT_KV_HEADS
ATT_WIDTH = ATT_HEADS * HEAD_DIM
KV_WIDTH = ATT_KV_HEADS * HEAD_DIM
GLA_HEADS = 4
GLA_DK = 64
GLA_DV = 128
GLA_KW = GLA_HEADS * GLA_DK
GLA_VW = GLA_HEADS * GLA_DV
GLA_RANK = 16
GLA_TAU = 16.0
ROPE_THETA = 10000.0
LOG2_E = 1.4426950408889634
ATT_DIRECT_EXP_LIMIT = 48.0
ATT_LOOKAHEAD = 2
EPS = 1e-6

COL_Q = 0
COL_K = 1024
COL_V = 1280
COL_AG = 1536
COL_CH = 2560
COL_CB = 3072
COL_CC = 3584
COL_CG = 4096
COL_GQK = 4608
COL_GV = 5120
COL_GG = 5632
COL_LR = 6144
IN_WIDTH = 6176
IN_TILE = 1024
IN_ROW_CHUNKS = 4
IN_MAIN = COL_LR
LR_BLOCK = 128

GLA_CHUNK = 128
GLA_LEVELS = (16, 32, 64, 128)
GLA_BLOCK = 256
GLA_EXP_CLAMP = 60.0

VMEM_LIMIT = 56 * 1024 * 1024


def _silu(x):
    return x * jax.nn.sigmoid(x)


def _rms(x, w):
    return x * lax.rsqrt(jnp.mean(x * x, axis=-1, keepdims=True) + EPS) * w


def _rope(y, cos, sin_signed):
    lane = lax.broadcasted_iota(jnp.int32, y.shape, 1)
    partner = jnp.where((lane & 63) < 32, pltpu.roll(y, 96, axis=1), pltpu.roll(y, 32, axis=1))
    return y * cos + partner * sin_signed


def _mod_kernel(c_ref, w_ref, b_ref, o_ref):
    a = _silu(c_ref[...]).astype(BF16)
    o_ref[0] = jnp.dot(a, w_ref[0].astype(BF16), preferred_element_type=F32) + b_ref[0]


def _modulation(cond, w_mod, b_mod):
    depth, d, n = w_mod.shape
    tn = 768
    return pl.pallas_call(
        _mod_kernel,
        grid=(depth, n // tn),
        in_specs=[pl.BlockSpec((8, d), lambda l, j: (0, 0)),
                  pl.BlockSpec((1, d, tn), lambda l, j: (l, 0, j)),
                  pl.BlockSpec((1, 1, tn), lambda l, j: (l, 0, j))],
        out_specs=pl.BlockSpec((1, 8, tn), lambda l, j: (l, 0, j)),
        out_shape=jax.ShapeDtypeStruct((depth, 8, n), F32),
        compiler_params=pltpu.CompilerParams(dimension_semantics=("parallel", "parallel"),
                                             vmem_limit_bytes=VMEM_LIMIT),
        name="modulation",
    )(cond, w_mod, b_mod.reshape(depth, 1, n))


_NT = (((1,), (1,)), ((), ()))


def _in_kernel(x_ref, nw_ref, shift_ref, scale_ref, w_ref, wlr_ref, o_ref, lr_ref, xn_ref):
    def project(xn, w):
        return lax.dot_general(xn, w, _NT, preferred_element_type=F32).astype(BF16)

    @pl.when(pl.program_id(1) == 0)
    def _():
        gain = nw_ref[...] * (1.0 + scale_ref[0])
        per = x_ref.shape[0] // IN_ROW_CHUNKS
        for r in range(IN_ROW_CHUNKS):
            rows = slice(r * per, (r + 1) * per)
            x = x_ref[rows, :]
            inv = lax.rsqrt(jnp.mean(x * x, axis=-1, keepdims=True) + EPS)
            xn = (x * inv * gain + shift_ref[0]).astype(BF16)
            xn_ref[rows, :] = xn
            o_ref[rows, :] = project(xn, w_ref[0])
            lr_ref[rows, :] = project(xn, wlr_ref[0])

    @pl.when(pl.program_id(1) > 0)
    def _():
        o_ref[...] = project(xn_ref[...], w_ref[0])


def _in_projection(x, mod3, norm_w, w_in_t, w_lr_t, layer, mod_row):
    t, d = x.shape
    tm = 1024
    return pl.pallas_call(
        _in_kernel,
        grid=(t // tm, IN_MAIN // IN_TILE),
        in_specs=[pl.BlockSpec((tm, d), lambda i, j: (i, 0)),
                  pl.BlockSpec((1, d), lambda i, j: (0, 0)),
                  pl.BlockSpec((1, 1, d), lambda i, j: (mod_row(i * tm), 0, 0)),
                  pl.BlockSpec((1, 1, d), lambda i, j: (mod_row(i * tm), 0, 1)),
                  pl.BlockSpec((1, IN_TILE, d), lambda i, j: (layer, j, 0)),
                  pl.BlockSpec((1, LR_BLOCK, d), lambda i, j: (layer, 0, 0))],
        out_specs=[pl.BlockSpec((tm, IN_TILE), lambda i, j: (i, j)),
                   pl.BlockSpec((tm, LR_BLOCK), lambda i, j: (i, 0))],
        out_shape=[jax.ShapeDtypeStruct((t, IN_MAIN), BF16),
                   jax.ShapeDtypeStruct((t, LR_BLOCK), BF16)],
        scratch_shapes=[pltpu.VMEM((tm, d), BF16)],
        compiler_params=pltpu.CompilerParams(dimension_semantics=("parallel", "arbitrary"),
                                             vmem_limit_bytes=VMEM_LIMIT),
        name="in_projection",
    )(x, norm_w.reshape(1, d), mod3, mod3, w_in_t, w_lr_t)


def _kv_kernel(kv_ref, kw_ref, cos_ref, sin_ref, ck_ref, cv_ref, k_ref, vt_ref, kmax_ref, *, n_self):
    @pl.when(pl.program_id(1) == 0)
    def _():
        kmax_ref[...] = jnp.zeros(kmax_ref.shape, F32)

    def put_keys(g, kb):
        cols = slice(g * HEAD_DIM, (g + 1) * HEAD_DIM)
        k_ref[:, cols] = kb
        kf = kb.astype(F32)
        n2 = jnp.max(jnp.sum(kf * kf, axis=1, keepdims=True), axis=0, keepdims=True)
        kmax_ref[0, g:g + 1, :] = jnp.maximum(kmax_ref[0, g:g + 1, :], jnp.broadcast_to(n2, (1, HEAD_DIM)))

    @pl.when(pl.program_id(1) < n_self)
    def _():
        for g in range(ATT_KV_HEADS):
            cols = slice(g * HEAD_DIM, (g + 1) * HEAD_DIM)
            y = _rope(_rms(kv_ref[:, cols].astype(F32), kw_ref[...]), cos_ref[...], sin_ref[...])
            put_keys(g, y.astype(BF16))
        vt_ref[...] = kv_ref[:, KV_WIDTH:].astype(F32).T.astype(BF16)

    @pl.when(pl.program_id(1) >= n_self)
    def _():
        for g in range(ATT_KV_HEADS):
            put_keys(g, ck_ref[:, g * HEAD_DIM:(g + 1) * HEAD_DIM].astype(BF16))
        vt_ref[...] = cv_ref[...].T.astype(BF16)


def _kv_prepare(proj, k_norm_w, rope_tables, ctx_kv, batch, seq_len):
    tr = 512
    n_self = seq_len // tr
    n_ctx = ctx_kv[0].shape[0] // batch // tr
    n_all = n_self + n_ctx
    own = lambda j: jnp.minimum(j, n_self - 1)
    ctx_spec = pl.BlockSpec((tr, KV_WIDTH), lambda b, j: (b * n_ctx + jnp.maximum(j - n_self, 0), 0))
    rope_spec = pl.BlockSpec((tr, HEAD_DIM), lambda b, j: (own(j), 0))
    return pl.pallas_call(
        functools.partial(_kv_kernel, n_self=n_self),
        grid=(batch, n_all),
        in_specs=[pl.BlockSpec((tr, 2 * KV_WIDTH), lambda b, j: (b * n_self + own(j), COL_K // (2 * KV_WIDTH))),
                  pl.BlockSpec((1, HEAD_DIM), lambda b, j: (0, 0)),
                  rope_spec, rope_spec, ctx_spec, ctx_spec],
        out_specs=[pl.BlockSpec((tr, KV_WIDTH), lambda b, j: (b * n_all + j, 0)),
                   pl.BlockSpec((KV_WIDTH, tr), lambda b, j: (b, j)),
                   pl.BlockSpec((1, 8, HEAD_DIM), lambda b, j: (b, 0, 0))],
        out_shape=[jax.ShapeDtypeStruct((batch * n_all * tr, KV_WIDTH), BF16),
                   jax.ShapeDtypeStruct((batch * KV_WIDTH, n_all * tr), BF16),
                   jax.ShapeDtypeStruct((batch, 8, HEAD_DIM), F32)],
        compiler_params=pltpu.CompilerParams(dimension_semantics=("parallel", "arbitrary")),
        name="kv_prepare",
    )(proj, k_norm_w.reshape(1, HEAD_DIM), *rope_tables, *ctx_kv)


def _attn_kernel(q_ref, gate_ref, k_ref, vt_ref, kmax_ref, qw_ref, cos_ref, sin_ref,
                 o_ref, qt_ref, s0_ref, s1_ref, m_ref, l_ref, acc_ref, *, tq, tk, n_chunks):
    q_scale = HEAD_DIM ** -0.5 * LOG2_E
    heads = [slice(h * tq, (h + 1) * tq) for h in range(HEADS_PER_KV)]
    dims = [slice(h * HEAD_DIM, (h + 1) * HEAD_DIM) for h in range(HEADS_PER_KV)]

    def rows(c):
        off = c * tk
        return pl.ds(off if isinstance(off, int) else pl.multiple_of(off, tk), tk)

    def prepare(h):
        y = _rope(_rms(q_ref[:, dims[h]].astype(F32), qw_ref[...]), cos_ref[...], sin_ref[...])
        qt_ref[:, heads[h]] = (y * q_scale).T.astype(BF16)
        l_ref[:, heads[h]] = jnp.zeros((1, tq), F32)
        acc_ref[:, heads[h]] = jnp.zeros((HEAD_DIM, tq), F32)

    def finish(h):
        o = (acc_ref[:, heads[h]] / l_ref[:, heads[h]]).T
        o_ref[:, dims[h]] = (o * _silu(gate_ref[:, dims[h]].astype(F32))).astype(BF16)

    w = qw_ref[...]
    q_norm2 = HEAD_DIM * q_scale * q_scale * jnp.max(w * w)
    key_norm2 = jnp.max(kmax_ref[0, pl.ds(pl.program_id(1), 1), :])
    bounded = q_norm2 * key_norm2 < ATT_DIRECT_EXP_LIMIT ** 2

    @pl.when(bounded)
    def _():
        pieces = [(h, c) for h in range(HEADS_PER_KV) for c in range(n_chunks)]
        pending = {}
        for idx in range(len(pieces) + ATT_LOOKAHEAD):
            if idx < len(pieces):
                h, c = pieces[idx]
                if c == 0:
                    prepare(h)
                pending[idx] = jnp.dot(k_ref[rows(c), :], qt_ref[:, heads[h]], preferred_element_type=F32)
            if idx >= ATT_LOOKAHEAD:
                h, c = pieces[idx - ATT_LOOKAHEAD]
                p = jnp.exp2(pending.pop(idx - ATT_LOOKAHEAD))
                l_ref[:, heads[h]] += jnp.sum(p, axis=0, keepdims=True)
                acc_ref[:, heads[h]] += jnp.dot(vt_ref[:, rows(c)], p.astype(BF16), preferred_element_type=F32)
                if c == n_chunks - 1:
                    finish(h)

    @pl.when(jnp.logical_not(bounded))
    def _():
        for h in range(HEADS_PER_KV):
            prepare(h)
        m_ref[...] = jnp.full(m_ref.shape, -jnp.inf, F32)

        def scores(c, s_ref):
            s_ref[...] = jnp.dot(k_ref[rows(c), :], qt_ref[...], preferred_element_type=F32)

        def update(c, s_ref):
            s = s_ref[...]
            m_old = m_ref[...]
            m_new = jnp.maximum(m_old, jnp.max(s, axis=0, keepdims=True))
            p = jnp.exp2(s - m_new)
            alpha = jnp.exp2(m_old - m_new)
            l_ref[...] = alpha * l_ref[...] + jnp.sum(p, axis=0, keepdims=True)
            pv = jnp.dot(vt_ref[:, rows(c)], p.astype(BF16), preferred_element_type=F32)
            acc_ref[...] = alpha * acc_ref[...] + pv
            m_ref[...] = m_new

        scores(0, s0_ref)
        n_pairs = (n_chunks - 1) // 2

        def body(i, carry):
            c = 2 * i
            scores(c + 1, s1_ref)
            update(c, s0_ref)
            scores(c + 2, s0_ref)
            update(c + 1, s1_ref)
            return carry

        lax.fori_loop(0, n_pairs, body, 0)
        done = 2 * n_pairs
        if n_chunks - done == 2:
            scores(done + 1, s1_ref)
            update(done, s0_ref)
            update(done + 1, s1_ref)
        else:
            update(done, s0_ref)
        for h in range(HEADS_PER_KV):
            finish(h)


def _attention(proj, k_all, vt_all, k_norm2, q_norm_w, rope_tables, batch, seq_len):
    t = proj.shape[0]
    keys = k_all.shape[0] // batch
    tq = 256
    tk = min(512, keys)
    nq = seq_len // tq
    gw = HEADS_PER_KV * HEAD_DIM
    rope_spec = pl.BlockSpec((tq, HEAD_DIM), lambda b, g, i: (i, 0))
    in_specs = [pl.BlockSpec((tq, gw), lambda b, g, i: (b * nq + i, COL_Q // gw + g)),
                pl.BlockSpec((tq, gw), lambda b, g, i: (b * nq + i, COL_AG // gw + g)),
                pl.BlockSpec((keys, HEAD_DIM), lambda b, g, i: (b, g)),
                pl.BlockSpec((HEAD_DIM, keys), lambda b, g, i: (b * ATT_KV_HEADS + g, 0)),
                pl.BlockSpec((1,) + k_norm2.shape[1:], lambda b, g, i: (b, 0, 0)),
                pl.BlockSpec((1, HEAD_DIM), lambda b, g, i: (0, 0)),
                rope_spec, rope_spec]
    args = [proj, proj, k_all, vt_all, k_norm2, q_norm_w.reshape(1, HEAD_DIM), *rope_tables]
    return pl.pallas_call(
        functools.partial(_attn_kernel, tq=tq, tk=tk, n_chunks=keys // tk),
        grid=(batch, ATT_KV_HEADS, nq),
        in_specs=in_specs,
        out_specs=pl.BlockSpec((tq, gw), lambda b, g, i: (b * nq + i, g)),
        out_shape=jax.ShapeDtypeStruct((t, ATT_WIDTH), BF16),
        scratch_shapes=[pltpu.VMEM((HEAD_DIM, HEADS_PER_KV * tq), BF16),
                        pltpu.VMEM((tk, HEADS_PER_KV * tq), F32),
                        pltpu.VMEM((tk, HEADS_PER_KV * tq), F32),
                        pltpu.VMEM((1, HEADS_PER_KV * tq), F32),
                        pltpu.VMEM((1, HEADS_PER_KV * tq), F32),
                        pltpu.VMEM((HEAD_DIM, HEADS_PER_KV * tq), F32)],
        compiler_params=pltpu.CompilerParams(dimension_semantics=("parallel", "parallel", "arbitrary"),
                                             vmem_limit_bytes=VMEM_LIMIT),
        name="attention",
    )(*args)


def _ctx_attn_kernel(q_ref, kv_ref, gate0_ref, gate1_ref, qw_ref, kw_ref, o_ref, kn_ref, vf_ref):
    q_scale = HEAD_DIM ** -0.5 * LOG2_E
    gates = (gate0_ref, gate1_ref)
    keys, values_t = [], []
    for g in range(ATT_KV_HEADS):
        dims = slice(g * HEAD_DIM, (g + 1) * HEAD_DIM)
        kn = _rms(kv_ref[:, dims].astype(F32), kw_ref[...])
        kn_ref[:, dims] = kn
        keys.append(kn.astype(BF16))
        v = kv_ref[:, KV_WIDTH + g * HEAD_DIM:KV_WIDTH + (g + 1) * HEAD_DIM].astype(F32)
        vf_ref[:, dims] = v
        values_t.append(v.T.astype(BF16))

    pending = {}
    for idx in range(ATT_HEADS + ATT_LOOKAHEAD):
        if idx < ATT_HEADS:
            dims = slice(idx * HEAD_DIM, (idx + 1) * HEAD_DIM)
            y = _rms(q_ref[:, dims].astype(F32), qw_ref[...]) * q_scale
            pending[idx] = jnp.dot(keys[idx // HEADS_PER_KV], y.T.astype(BF16), preferred_element_type=F32)
        head = idx - ATT_LOOKAHEAD
        if head >= 0:
            g, h = divmod(head, HEADS_PER_KV)
            s = pending.pop(head)
            p = jnp.exp2(s - jnp.max(s, axis=0, keepdims=True))
            l = jnp.sum(p, axis=0, keepdims=True)
            acc = jnp.dot(values_t[g], p.astype(BF16), preferred_element_type=F32)
            gate = gates[g][:, h * HEAD_DIM:(h + 1) * HEAD_DIM].astype(F32)
            o_ref[:, head * HEAD_DIM:(head + 1) * HEAD_DIM] = ((acc / l).T * _silu(gate)).astype(BF16)


def _context_attention(proj, q_norm_w, k_norm_w, batch, seq_len):
    t = proj.shape[0]
    gw = HEADS_PER_KV * HEAD_DIM
    row_block = lambda width, col: pl.BlockSpec((seq_len, width), lambda b: (b, col // width))
    vec = pl.BlockSpec((1, HEAD_DIM), lambda b: (0, 0))
    return pl.pallas_call(
        _ctx_attn_kernel,
        grid=(batch,),
        in_specs=[row_block(ATT_WIDTH, COL_Q), row_block(2 * KV_WIDTH, COL_K),
                  row_block(gw, COL_AG), row_block(gw, COL_AG + gw), vec, vec],
        out_specs=[row_block(ATT_WIDTH, 0), row_block(KV_WIDTH, 0), row_block(KV_WIDTH, 0)],
        out_shape=[jax.ShapeDtypeStruct((t, ATT_WIDTH), BF16),
                   jax.ShapeDtypeStruct((t, KV_WIDTH), F32),
                   jax.ShapeDtypeStruct((t, KV_WIDTH), F32)],
        compiler_params=pltpu.CompilerParams(dimension_semantics=("parallel",),
                                             vmem_limit_bytes=VMEM_LIMIT),
        name="context_attention",
    )(proj, proj, proj, proj, q_norm_w.reshape(1, HEAD_DIM), k_norm_w.reshape(1, HEAD_DIM))


def _log_sigmoid(x):
    return -(jnp.maximum(-x, 0.0) + jnp.log(1.0 + jnp.exp(-jnp.abs(x))))


def _gla_level_table():
    t = np.arange(GLA_CHUNK)[:, None]
    s = np.arange(GLA_CHUNK)[None, :]
    lvl = np.full((GLA_CHUNK, GLA_CHUNK), len(GLA_LEVELS), np.int32)
    for i, size in reversed(list(enumerate(GLA_LEVELS))):
        lvl = np.where(t // size == s // size, i, lvl)
    fwd = np.where(s <= t, lvl, len(GLA_LEVELS)).astype(np.int32)
    return jnp.asarray(np.stack([fwd, fwd.T]))


def _minus_rows(b, picks, size):
    return jnp.concatenate(
        [b[i * size:(i + 1) * size] if p is None else b[i * size:(i + 1) * size] - b[p:p + 1]
         for i, p in enumerate(picks)], axis=0)


def _gla_decays(lr, a, bias):
    logits = jnp.dot(lr, a, preferred_element_type=F32) + bias
    return _log_sigmoid(logits) * (1.0 / GLA_TAU)


def _gla_cumsum(g, lvl):
    g_hi = g.astype(BF16)
    g_lo = (g - g_hi.astype(F32)).astype(BF16)
    tri = jnp.where(lvl < len(GLA_LEVELS), 1.0, 0.0).astype(BF16)
    return jnp.dot(tri, g_hi, preferred_element_type=F32) + jnp.dot(tri, g_lo, preferred_element_type=F32)


def _gla_operands(qk, b, rev):
    c = GLA_CHUNK
    q = qk[:, :GLA_KW].astype(F32) * GLA_DK ** -0.5
    k = qk[:, GLA_KW:].astype(F32)
    b_total = b[0:1] if rev else b[c - 1:c]
    q_inter = (q * jnp.exp(b)).astype(BF16)
    k_state = (k * jnp.exp(b_total - b)).astype(BF16)
    decay_total = jnp.exp(b_total)

    row = lax.broadcasted_iota(jnp.int32, (c, GLA_KW), 0)
    pairs = []
    for size in GLA_LEVELS:
        n = c // size
        if size == GLA_LEVELS[0]:
            if rev:
                picks = [(i + 1) * size if i + 1 < n else None for i in range(n)]
            else:
                picks = [i * size - 1 if i > 0 else None for i in range(n)]
            d = _minus_rows(b, picks, size)
            q_hat = q * jnp.exp(d)
            k_hat = k * jnp.exp(jnp.minimum(-d, GLA_EXP_CLAMP))
        else:
            half = size // 2
            d = _minus_rows(b, [i * size + half - (0 if rev else 1) for i in range(n)], size)
            late = (row & half) != 0
            q_side = jnp.logical_not(late) if rev else late
            q_hat = q * jnp.exp(jnp.where(q_side, d, -jnp.inf))
            k_hat = k * jnp.exp(jnp.where(q_side, -jnp.inf, -d))
        pairs.append((q_hat.astype(BF16), k_hat.astype(BF16)))
    return q_inter, k_state, decay_total, pairs


def _gla_kernel(qkf_ref, vf_ref, lrf_ref, qkb_ref, vb_ref, lrb_ref, a_ref, bias_ref, lvl_ref, s0_ref,
                of_ref, ob_ref, so_ref, st_ref):
    j = pl.program_id(1)

    @pl.when(j == 0)
    def _():
        for z in range(2):
            for h in range(GLA_HEADS):
                st_ref[z, h] = s0_ref[0, z, h].T

    n_chunks = GLA_BLOCK // GLA_CHUNK
    chunk_rows = [slice(c * GLA_CHUNK, (c + 1) * GLA_CHUNK) for c in range(n_chunks)]
    units = [(z, rows) for rows_f, rows_b in zip(chunk_rows, reversed(chunk_rows))
             for z, rows in ((0, rows_f), (1, rows_b))]
    src = ((qkf_ref, vf_ref, lrf_ref, of_ref), (qkb_ref, vb_ref, lrb_ref, ob_ref))
    n_levels = len(GLA_LEVELS)
    hk = [slice(h * GLA_DK, (h + 1) * GLA_DK) for h in range(GLA_HEADS)]
    hv = [slice(h * GLA_DV, (h + 1) * GLA_DV) for h in range(GLA_HEADS)]
    tn = (((0,), (0,)), ((), ()))

    g = [_gla_decays(src[z][2][rows, :], a_ref[z], bias_ref[z]) for z, rows in units]
    b = [_gla_cumsum(gi, lvl_ref[z]) for gi, (z, rows) in zip(g, units)]
    v = [src[z][1][rows, :] for z, rows in units]
    state = [[st_ref[z, h] for h in range(GLA_HEADS)] for z in range(2)]
    ops, delta, scores = {}, {}, {}

    def queue_scores(u):
        z, rows = units[u]
        ops[u] = _gla_operands(src[z][0][rows, :], b[u], z == 1)
        delta[u] = [lax.dot_general(v[u][:, hv[h]], ops[u][1][:, hk[h]], tn, preferred_element_type=F32)
                    for h in range(GLA_HEADS)]
        scores[u] = [[lax.dot_general(qh[:, hk[h]], kh[:, hk[h]], _NT, preferred_element_type=F32)
                      for qh, kh in ops[u][3]] for h in range(GLA_HEADS)]

    def consume(u):
        z, rows = units[u]
        lvl = lvl_ref[z]
        q_inter, _, decay_total, _ = ops.pop(u)
        for h in range(GLA_HEADS):
            p = scores[u][h]
            att = p[-1]
            for i in reversed(range(n_levels - 1)):
                att = jnp.where(lvl == i, p[i], att)
            o = jnp.dot(att.astype(BF16), v[u][:, hv[h]], preferred_element_type=F32) + lax.dot_general(
                q_inter[:, hk[h]], state[z][h].astype(BF16), _NT, preferred_element_type=F32)
            src[z][3][rows, hv[h]] = o.astype(BF16)
            state[z][h] = state[z][h] * decay_total[:, hk[h]] + delta[u][h]

    for u in range(len(units)):
        queue_scores(u)
        if u > 0:
            consume(u - 1)
    consume(len(units) - 1)
    for z in range(2):
        for h in range(GLA_HEADS):
            st_ref[z, h] = state[z][h]

    @pl.when(j == pl.num_programs(1) - 1)
    def _():
        for z in range(2):
            for h in range(GLA_HEADS):
                so_ref[0, z, h] = st_ref[z, h].T


def _gla(proj, lr, a_pad, bias, s0, batch, seq_len):
    t = proj.shape[0]
    nb = seq_len // GLA_BLOCK
    fwd = lambda b, j: b * nb + j
    bwd = lambda b, j: b * nb + nb - 1 - j
    qk_w = 2 * GLA_KW

    def proj_specs(row):
        return [pl.BlockSpec((GLA_BLOCK, qk_w), lambda b, j: (row(b, j), COL_GQK // qk_w)),
                pl.BlockSpec((GLA_BLOCK, GLA_VW), lambda b, j: (row(b, j), COL_GV // GLA_VW)),
                pl.BlockSpec((GLA_BLOCK, LR_BLOCK), lambda b, j: (row(b, j), 0))]

    state_spec = pl.BlockSpec((1, 2, GLA_HEADS, GLA_DK, GLA_DV), lambda b, j: (b, 0, 0, 0, 0))
    return pl.pallas_call(
        _gla_kernel,
        grid=(batch, nb),
        in_specs=proj_specs(fwd) + proj_specs(bwd) + [
            pl.BlockSpec((2, LR_BLOCK, GLA_KW), lambda b, j: (0, 0, 0)),
            pl.BlockSpec((2, 1, GLA_KW), lambda b, j: (0, 0, 0)),
            pl.BlockSpec((2, GLA_CHUNK, GLA_CHUNK), lambda b, j: (0, 0, 0)),
            state_spec],
        out_specs=[pl.BlockSpec((GLA_BLOCK, GLA_VW), lambda b, j: (fwd(b, j), 0)),
                   pl.BlockSpec((GLA_BLOCK, GLA_VW), lambda b, j: (bwd(b, j), 0)),
                   state_spec],
        out_shape=[jax.ShapeDtypeStruct((t, GLA_VW), BF16),
                   jax.ShapeDtypeStruct((t, GLA_VW), BF16),
                   jax.ShapeDtypeStruct((batch, 2, GLA_HEADS, GLA_DK, GLA_DV), F32)],
        scratch_shapes=[pltpu.VMEM((2, GLA_HEADS, GLA_DV, GLA_DK), F32)],
        compiler_params=pltpu.CompilerParams(dimension_semantics=("parallel", "arbitrary"),
                                             vmem_limit_bytes=VMEM_LIMIT),
        name="gla",
    )(proj, proj, lr, proj, proj, lr, a_pad, bias, _gla_level_table(), s0)


OUT_ROW_CHUNK = 128
HALO = 16


def _out_kernel(*refs, tm, seq_len, final):
    (x_ref, a_ref, ch_ref, cb_ref, cc_ref, cg_ref, chp_ref, ccp_ref, chn_ref, ccn_ref,
     gg_ref, of_ref, ob_ref, cw_ref, gnw_ref, gate_ref, w_ref) = refs[:17]
    pos = 17
    if final:
        fnw_ref = refs[pos]
        pos += 1
    o_ref = refs[pos]

    def prod(c_ref, h_ref, r):
        return c_ref[r:r + 1, :].astype(F32) * h_ref[r:r + 1, :].astype(F32)

    tc = OUT_ROW_CHUNK

    for lo in range(0, tm, tc):
        rows = slice(lo, lo + tc)

        u = cc_ref[rows, :].astype(F32) * ch_ref[rows, :].astype(F32)
        u_before = prod(ccp_ref, chp_ref, HALO - 1) if lo == 0 else prod(cc_ref, ch_ref, lo - 1)
        u_after = prod(ccn_ref, chn_ref, 0) if lo + tc == tm else prod(cc_ref, ch_ref, lo + tc)
        row = lax.broadcasted_iota(jnp.int32, u.shape, 0)
        seq_pos = (pl.program_id(0) * tm + lo + row) & (seq_len - 1)
        u_prev = jnp.where(row == 0, u_before, pltpu.roll(u, 1, axis=0))
        u_prev = jnp.where(seq_pos == 0, 0.0, u_prev)
        u_next = jnp.where(row == tc - 1, u_after, pltpu.roll(u, tc - 1, axis=0))
        u_next = jnp.where(seq_pos == seq_len - 1, 0.0, u_next)
        conv = u_prev * cw_ref[0:1, :] + u * cw_ref[1:2, :] + u_next * cw_ref[2:3, :]
        out_b = cb_ref[rows, :].astype(F32) * conv * _silu(cg_ref[rows, :].astype(F32))
        parts = [out_b.astype(BF16)]

        o = of_ref[rows, :].astype(F32) + ob_ref[rows, :].astype(F32)
        gg = gg_ref[rows, :].astype(F32)
        for h in range(GLA_HEADS):
            cols = slice(h * GLA_DV, (h + 1) * GLA_DV)
            parts.append((_rms(o[:, cols], gnw_ref[...]) * _silu(gg[:, cols])).astype(BF16))

        r = jnp.dot(a_ref[rows, :], w_ref[0, :ATT_WIDTH, :], preferred_element_type=F32)
        r = r + jnp.dot(jnp.concatenate(parts, axis=1), w_ref[0, ATT_WIDTH:, :], preferred_element_type=F32)
        xn = x_ref[rows, :] + gate_ref[0] * r
        if final:
            xn = _rms(xn, fnw_ref[...])
        o_ref[rows, :] = xn


def _out_projection(x, att, proj, o_f, o_b, conv_w, gla_norm_w, mod3, w_out, layer, final_norm_w, mod_row,
                    seq_len):
    t, d = x.shape
    tm = 512
    cw = conv_w.shape[1]
    final = final_norm_w is not None
    per_halo = tm // HALO
    last_halo = t // HALO - 1

    def col(c, width):
        return lambda i: (i, c // width)

    prev = lambda c: (lambda i: (jnp.maximum(i * per_halo - 1, 0), c // cw))
    nxt = lambda c: (lambda i: (jnp.minimum((i + 1) * per_halo, last_halo), c // cw))
    in_specs = [pl.BlockSpec((tm, d), lambda i: (i, 0)),
                pl.BlockSpec((tm, ATT_WIDTH), lambda i: (i, 0)),
                pl.BlockSpec((tm, cw), col(COL_CH, cw)),
                pl.BlockSpec((tm, cw), col(COL_CB, cw)),
                pl.BlockSpec((tm, cw), col(COL_CC, cw)),
                pl.BlockSpec((tm, cw), col(COL_CG, cw)),
                pl.BlockSpec((HALO, cw), prev(COL_CH)),
                pl.BlockSpec((HALO, cw), prev(COL_CC)),
                pl.BlockSpec((HALO, cw), nxt(COL_CH)),
                pl.BlockSpec((HALO, cw), nxt(COL_CC)),
                pl.BlockSpec((tm, GLA_VW), col(COL_GG, GLA_VW)),
                pl.BlockSpec((tm, GLA_VW), lambda i: (i, 0)),
                pl.BlockSpec((tm, GLA_VW), lambda i: (i, 0)),
                pl.BlockSpec((3, cw), lambda i: (0, 0)),
                pl.BlockSpec((1, GLA_DV), lambda i: (0, 0)),
                pl.BlockSpec((1, 1, d), lambda i: (mod_row(i * tm), 0, 2)),
                pl.BlockSpec((1,) + w_out.shape[1:], lambda i: (layer, 0, 0))]
    args = [x, att, proj, proj, proj, proj, proj, proj, proj, proj, proj, o_f, o_b,
            conv_w, gla_norm_w.reshape(1, GLA_DV), mod3, w_out]
    if final:
        in_specs.append(pl.BlockSpec((1, d), lambda i: (0, 0)))
        args.append(final_norm_w.reshape(1, d))
    return pl.pallas_call(
        functools.partial(_out_kernel, tm=tm, seq_len=seq_len, final=final),
        grid=(t // tm,),
        in_specs=in_specs,
        out_specs=pl.BlockSpec((tm, d), lambda i: (i, 0)),
        out_shape=jax.ShapeDtypeStruct((t, d), F32),
        compiler_params=pltpu.CompilerParams(dimension_semantics=("parallel",),
                                             vmem_limit_bytes=VMEM_LIMIT),
        name="out_projection",
    )(*args)


def _rope_tables(seq_len):
    pos = jnp.arange(seq_len)
    row = (pos // GRID_W).astype(F32)
    col = (pos % GRID_W).astype(F32)
    n_freq = HEAD_DIM // 4
    inv = ROPE_THETA ** (-jnp.arange(n_freq, dtype=F32) / n_freq)
    ang_r = row[:, None] * inv[None, :]
    ang_c = col[:, None] * inv[None, :]
    cos = jnp.concatenate([jnp.cos(ang_r), jnp.cos(ang_r), jnp.cos(ang_c), jnp.cos(ang_c)], axis=-1)
    sin = jnp.concatenate([-jnp.sin(ang_r), jnp.sin(ang_r), -jnp.sin(ang_c), jnp.sin(ang_c)], axis=-1)
    return cos, sin


@jax.jit
def _forward(x_prompt, x_sample, cache_k, cache_v, state_gla, c, c_ctx, w_mod, b_mod, norm_w, w_in,
             q_norm_w, k_norm_w, conv_w, gla_a_up, gla_a_bias, gla_norm_w, w_out, final_norm_w):
    batch, seq, d = x_prompt.shape
    dec_batch, dec_seq, _ = x_sample.shape
    depth = w_in.shape[0]
    past = cache_k.shape[2]

    cond = jnp.zeros((8, d), F32).at[0].set(c_ctx).at[1:1 + dec_batch].set(c)
    mod3 = _modulation(cond, w_mod, b_mod).reshape(depth * 8, 1, 3 * d)
    rope_tables = _rope_tables(dec_seq)

    h = x_prompt.reshape(batch * seq, d)
    z = x_sample.reshape(dec_batch * dec_seq, d)
    zero_state = jnp.zeros((batch, 2, GLA_HEADS, GLA_DK, GLA_DV), F32)
    w_in_t = jnp.swapaxes(w_in, 1, 2).astype(BF16)
    w_lr_t = jnp.pad(w_in_t[:, COL_LR:, :], ((0, 0), (0, LR_BLOCK - (IN_WIDTH - COL_LR)), (0, 0)))
    w_out_b = w_out.astype(BF16)
    new_k, new_v, new_state = [], [], []
    for l in range(depth):
        a_pad = jnp.zeros((2, LR_BLOCK, GLA_KW), F32)
        for zdir in range(2):
            a_pad = a_pad.at[zdir, zdir * GLA_RANK:(zdir + 1) * GLA_RANK].set(gla_a_up[l, zdir])
        a_pad = a_pad.astype(BF16)
        bias = gla_a_bias[l].reshape(2, 1, GLA_KW)
        final_w = final_norm_w if l == depth - 1 else None

        row_ctx = lambda tok, l=l: l * 8
        proj, lr = _in_projection(h, mod3, norm_w[l], w_in_t, w_lr_t, l, row_ctx)
        att, kn_f32, v_f32 = _context_attention(proj, q_norm_w[l], k_norm_w[l], batch, seq)
        o_f, o_b, s_out = _gla(proj, lr, a_pad, bias, zero_state, batch, seq)
        h = _out_projection(h, att, proj, o_f, o_b, conv_w[l], gla_norm_w[l], mod3, w_out_b, l, final_w,
                            row_ctx, seq)
        new_k.append(kn_f32.reshape(batch, seq, ATT_KV_HEADS, HEAD_DIM))
        new_v.append(v_f32.reshape(batch, seq, ATT_KV_HEADS, HEAD_DIM))
        new_state.append(s_out)

        row_lat = lambda tok, l=l: l * 8 + 1 + tok // dec_seq
        proj, lr = _in_projection(z, mod3, norm_w[l], w_in_t, w_lr_t, l, row_lat)
        ctx_kv = (cache_k[:, l].reshape(dec_batch * past, KV_WIDTH),
                  cache_v[:, l].reshape(dec_batch * past, KV_WIDTH))
        k_all, vt_all, k_norm2 = _kv_prepare(proj, k_norm_w[l], rope_tables, ctx_kv, dec_batch, dec_seq)
        att = _attention(proj, k_all, vt_all, k_norm2, q_norm_w[l], rope_tables, dec_batch, dec_seq)
        o_f, o_b, _ = _gla(proj, lr, a_pad, bias, state_gla[:, l], dec_batch, dec_seq)
        z = _out_projection(z, att, proj, o_f, o_b, conv_w[l], gla_norm_w[l], mod3, w_out_b, l, final_w,
                            row_lat, dec_seq)

    return (h.reshape(batch, seq, d), z.reshape(dec_batch, dec_seq, d),
            jnp.stack(new_k, axis=1), jnp.stack(new_v, axis=1), jnp.stack(new_state, axis=1))


def kernel(x_prompt, x_sample, cache_k, cache_v, state_gla, c, c_ctx, w_mod, b_mod, norm_w, w_in, q_norm_w,
           k_norm_w, conv_w, gla_a_up, gla_a_bias, gla_norm_w, w_out, final_norm_w):
    return _forward(x_prompt, x_sample, cache_k, cache_v, state_gla, c, c_ctx, w_mod, b_mod, norm_w, w_in,
                    q_norm_w, k_norm_w, conv_w, gla_a_up, gla_a_bias, gla_norm_w, w_out, final_norm_w)
```

```python
import functools

import numpy as np
import jax
import jax.numpy as jnp
from jax import lax
from jax.experimental import pallas as pl
from jax.experimental.pallas import tpu as pltpu

F32 = jnp.float32
BF16 = jnp.bfloat16

GRID_W = 64
HEAD_DIM = 128
ATT_HEADS = 8
ATT_KV_HEADS = 2
HEADS_PER_KV = ATT_HEADS // ATT_KV_HEADS
ATT_WIDTH = ATT_HEADS * HEAD_DIM
KV_WIDTH = ATT_KV_HEADS * HEAD_DIM
GLA_HEADS = 4
GLA_DK = 64
GLA_DV = 128
GLA_KW = GLA_HEADS * GLA_DK
GLA_VW = GLA_HEADS * GLA_DV
GLA_RANK = 16
GLA_TAU = 16.0
ROPE_THETA = 10000.0
LOG2_E = 1.4426950408889634
ATT_DIRECT_EXP_LIMIT = 48.0
ATT_LOOKAHEAD = 2
EPS = 1e-6

COL_Q = 0
COL_K = 1024
COL_V = 1280
COL_AG = 1536
COL_CH = 2560
COL_CB = 3072
COL_CC = 3584
COL_CG = 4096
COL_GQK = 4608
COL_GV = 5120
COL_GG = 5632
COL_LR = 6144
IN_WIDTH = 6176
IN_TILE = 1024
IN_ROW_CHUNKS = 4
IN_MAIN = COL_LR
LR_BLOCK = 128

GLA_CHUNK = 128
GLA_LEVELS = (16, 32, 64, 128)
GLA_BLOCK = 256
GLA_EXP_CLAMP = 60.0

VMEM_LIMIT = 56 * 1024 * 1024


def _silu(x):
    return x * jax.nn.sigmoid(x)


def _rms(x, w):
    return x * lax.rsqrt(jnp.mean(x * x, axis=-1, keepdims=True) + EPS) * w


def _rope(y, cos, sin_signed):
    lane = lax.broadcasted_iota(jnp.int32, y.shape, 1)
    partner = jnp.where((lane & 63) < 32, pltpu.roll(y, 96, axis=1), pltpu.roll(y, 32, axis=1))
    return y * cos + partner * sin_signed


def _mod_kernel(c_ref, w_ref, b_ref, o_ref):
    a = _silu(c_ref[...]).astype(BF16)
    o_ref[0] = jnp.dot(a, w_ref[0].astype(BF16), preferred_element_type=F32) + b_ref[0]


def _modulation(cond, w_mod, b_mod):
    depth, d, n = w_mod.shape
    tn = 768
    return pl.pallas_call(
        _mod_kernel,
        grid=(depth, n // tn),
        in_specs=[pl.BlockSpec((8, d), lambda l, j: (0, 0)),
                  pl.BlockSpec((1, d, tn), lambda l, j: (l, 0, j)),
                  pl.BlockSpec((1, 1, tn), lambda l, j: (l, 0, j))],
        out_specs=pl.BlockSpec((1, 8, tn), lambda l, j: (l, 0, j)),
        out_shape=jax.ShapeDtypeStruct((depth, 8, n), F32),
        compiler_params=pltpu.CompilerParams(dimension_semantics=("parallel", "parallel"),
                                             vmem_limit_bytes=VMEM_LIMIT),
        name="modulation",
    )(cond, w_mod, b_mod.reshape(depth, 1, n))


_NT = (((1,), (1,)), ((), ()))


def _in_kernel(x_ref, nw_ref, shift_ref, scale_ref, w_ref, wlr_ref, o_ref, lr_ref, xn_ref):
    def project(xn, w):
        return lax.dot_general(xn, w, _NT, preferred_element_type=F32).astype(BF16)

    @pl.when(pl.program_id(1) == 0)
    def _():
        gain = nw_ref[...] * (1.0 + scale_ref[0])
        per = x_ref.shape[0] // IN_ROW_CHUNKS
        for r in range(IN_ROW_CHUNKS):
            rows = slice(r * per, (r + 1) * per)
            x = x_ref[rows, :]
            inv = lax.rsqrt(jnp.mean(x * x, axis=-1, keepdims=True) + EPS)
            xn = (x * inv * gain + shift_ref[0]).astype(BF16)
            xn_ref[rows, :] = xn
            o_ref[rows, :] = project(xn, w_ref[0])
            lr_ref[rows, :] = project(xn, wlr_ref[0])

    @pl.when(pl.program_id(1) > 0)
    def _():
        o_ref[...] = project(xn_ref[...], w_ref[0])


def _in_projection(x, mod3, norm_w, w_in_t, w_lr_t, layer, mod_row):
    t, d = x.shape
    tm = 1024
    return pl.pallas_call(
        _in_kernel,
        grid=(t // tm, IN_MAIN // IN_TILE),
        in_specs=[pl.BlockSpec((tm, d), lambda i, j: (i, 0)),
                  pl.BlockSpec((1, d), lambda i, j: (0, 0)),
                  pl.BlockSpec((1, 1, d), lambda i, j: (mod_row(i * tm), 0, 0)),
                  pl.BlockSpec((1, 1, d), lambda i, j: (mod_row(i * tm), 0, 1)),
                  pl.BlockSpec((1, IN_TILE, d), lambda i, j: (layer, j, 0)),
                  pl.BlockSpec((1, LR_BLOCK, d), lambda i, j: (layer, 0, 0))],
        out_specs=[pl.BlockSpec((tm, IN_TILE), lambda i, j: (i, j)),
                   pl.BlockSpec((tm, LR_BLOCK), lambda i, j: (i, 0))],
        out_shape=[jax.ShapeDtypeStruct((t, IN_MAIN), BF16),
                   jax.ShapeDtypeStruct((t, LR_BLOCK), BF16)],
        scratch_shapes=[pltpu.VMEM((tm, d), BF16)],
        compiler_params=pltpu.CompilerParams(dimension_semantics=("parallel", "arbitrary"),
                                             vmem_limit_bytes=VMEM_LIMIT),
        name="in_projection",
    )(x, norm_w.reshape(1, d), mod3, mod3, w_in_t, w_lr_t)


def _kv_kernel(kv_ref, kw_ref, cos_ref, sin_ref, ck_ref, cv_ref, k_ref, vt_ref, kmax_ref, *, n_self):
    @pl.when(pl.program_id(1) == 0)
    def _():
        kmax_ref[...] = jnp.zeros(kmax_ref.shape, F32)

    def put_keys(g, kb):
        cols = slice(g * HEAD_DIM, (g + 1) * HEAD_DIM)
        k_ref[:, cols] = kb
        kf = kb.astype(F32)
        n2 = jnp.max(jnp.sum(kf * kf, axis=1, keepdims=True), axis=0, keepdims=True)
        kmax_ref[0, g:g + 1, :] = jnp.maximum(kmax_ref[0, g:g + 1, :], jnp.broadcast_to(n2, (1, HEAD_DIM)))

    @pl.when(pl.program_id(1) < n_self)
    def _():
        for g in range(ATT_KV_HEADS):
            cols = slice(g * HEAD_DIM, (g + 1) * HEAD_DIM)
            y = _rope(_rms(kv_ref[:, cols].astype(F32), kw_ref[...]), cos_ref[...], sin_ref[...])
            put_keys(g, y.astype(BF16))
        vt_ref[...] = kv_ref[:, KV_WIDTH:].astype(F32).T.astype(BF16)

    @pl.when(pl.program_id(1) >= n_self)
    def _():
        for g in range(ATT_KV_HEADS):
            put_keys(g, ck_ref[:, g * HEAD_DIM:(g + 1) * HEAD_DIM].astype(BF16))
        vt_ref[...] = cv_ref[...].T.astype(BF16)


def _kv_prepare(proj, k_norm_w, rope_tables, ctx_kv, batch, seq_len):
    tr = 512
    n_self = seq_len // tr
    n_ctx = ctx_kv[0].shape[0] // batch // tr
    n_all = n_self + n_ctx
    own = lambda j: jnp.minimum(j, n_self - 1)
    ctx_spec = pl.BlockSpec((tr, KV_WIDTH), lambda b, j: (b * n_ctx + jnp.maximum(j - n_self, 0), 0))
    rope_spec = pl.BlockSpec((tr, HEAD_DIM), lambda b, j: (own(j), 0))
    return pl.pallas_call(
        functools.partial(_kv_kernel, n_self=n_self),
        grid=(batch, n_all),
        in_specs=[pl.BlockSpec((tr, 2 * KV_WIDTH), lambda b, j: (b * n_self + own(j), COL_K // (2 * KV_WIDTH))),
                  pl.BlockSpec((1, HEAD_DIM), lambda b, j: (0, 0)),
                  rope_spec, rope_spec, ctx_spec, ctx_spec],
        out_specs=[pl.BlockSpec((tr, KV_WIDTH), lambda b, j: (b * n_all + j, 0)),
                   pl.BlockSpec((KV_WIDTH, tr), lambda b, j: (b, j)),
                   pl.BlockSpec((1, 8, HEAD_DIM), lambda b, j: (b, 0, 0))],
        out_shape=[jax.ShapeDtypeStruct((batch * n_all * tr, KV_WIDTH), BF16),
                   jax.ShapeDtypeStruct((batch * KV_WIDTH, n_all * tr), BF16),
                   jax.ShapeDtypeStruct((batch, 8, HEAD_DIM), F32)],
        compiler_params=pltpu.CompilerParams(dimension_semantics=("parallel", "arbitrary")),
        name="kv_prepare",
    )(proj, k_norm_w.reshape(1, HEAD_DIM), *rope_tables, *ctx_kv)


def _attn_kernel(q_ref, gate_ref, k_ref, vt_ref, kmax_ref, qw_ref, cos_ref, sin_ref,
                 o_ref, qt_ref, s0_ref, s1_ref, m_ref, l_ref, acc_ref, *, tq, tk, n_chunks):
    q_scale = HEAD_DIM ** -0.5 * LOG2_E
    heads = [slice(h * tq, (h + 1) * tq) for h in range(HEADS_PER_KV)]
    dims = [slice(h * HEAD_DIM, (h + 1) * HEAD_DIM) for h in range(HEADS_PER_KV)]

    def rows(c):
        off = c * tk
        return pl.ds(off if isinstance(off, int) else pl.multiple_of(off, tk), tk)

    def prepare(h):
        y = _rope(_rms(q_ref[:, dims[h]].astype(F32), qw_ref[...]), cos_ref[...], sin_ref[...])
        qt_ref[:, heads[h]] = (y * q_scale).T.astype(BF16)
        l_ref[:, heads[h]] = jnp.zeros((1, tq), F32)
        acc_ref[:, heads[h]] = jnp.zeros((HEAD_DIM, tq), F32)

    def finish(h):
        o = (acc_ref[:, heads[h]] / l_ref[:, heads[h]]).T
        o_ref[:, dims[h]] = (o * _silu(gate_ref[:, dims[h]].astype(F32))).astype(BF16)

    w = qw_ref[...]
    q_norm2 = HEAD_DIM * q_scale * q_scale * jnp.max(w * w)
    key_norm2 = jnp.max(kmax_ref[0, pl.ds(pl.program_id(1), 1), :])
    bounded = q_norm2 * key_norm2 < ATT_DIRECT_EXP_LIMIT ** 2

    @pl.when(bounded)
    def _():
        pieces = [(h, c) for h in range(HEADS_PER_KV) for c in range(n_chunks)]
        pending = {}
        for idx in range(len(pieces) + ATT_LOOKAHEAD):
            if idx < len(pieces):
                h, c = pieces[idx]
                if c == 0:
                    prepare(h)
                pending[idx] = jnp.dot(k_ref[rows(c), :], qt_ref[:, heads[h]], preferred_element_type=F32)
            if idx >= ATT_LOOKAHEAD:
                h, c = pieces[idx - ATT_LOOKAHEAD]
                p = jnp.exp2(pending.pop(idx - ATT_LOOKAHEAD))
                l_ref[:, heads[h]] += jnp.sum(p, axis=0, keepdims=True)
                acc_ref[:, heads[h]] += jnp.dot(vt_ref[:, rows(c)], p.astype(BF16), preferred_element_type=F32)
                if c == n_chunks - 1:
                    finish(h)

    @pl.when(jnp.logical_not(bounded))
    def _():
        for h in range(HEADS_PER_KV):
            prepare(h)
        m_ref[...] = jnp.full(m_ref.shape, -jnp.inf, F32)

        def scores(c, s_ref):
            s_ref[...] = jnp.dot(k_ref[rows(c), :], qt_ref[...], preferred_element_type=F32)

        def update(c, s_ref):
            s = s_ref[...]
            m_old = m_ref[...]
            m_new = jnp.maximum(m_old, jnp.max(s, axis=0, keepdims=True))
            p = jnp.exp2(s - m_new)
            alpha = jnp.exp2(m_old - m_new)
            l_ref[...] = alpha * l_ref[...] + jnp.sum(p, axis=0, keepdims=True)
            pv = jnp.dot(vt_ref[:, rows(c)], p.astype(BF16), preferred_element_type=F32)
            acc_ref[...] = alpha * acc_ref[...] + pv
            m_ref[...] = m_new

        scores(0, s0_ref)
        n_pairs = (n_chunks - 1) // 2

        def body(i, carry):
            c = 2 * i
            scores(c + 1, s1_ref)
            update(c, s0_ref)
            scores(c + 2, s0_ref)
            update(c + 1, s1_ref)
            return carry

        lax.fori_loop(0, n_pairs, body, 0)
        done = 2 * n_pairs
        if n_chunks - done == 2:
            scores(done + 1, s1_ref)
            update(done, s0_ref)
            update(done + 1, s1_ref)
        else:
            update(done, s0_ref)
        for h in range(HEADS_PER_KV):
            finish(h)


def _attention(proj, k_all, vt_all, k_norm2, q_norm_w, rope_tables, batch, seq_len):
    t = proj.shape[0]
    keys = k_all.shape[0] // batch
    tq = 512
    tk = min(512, keys)
    nq = seq_len // tq
    gw = HEADS_PER_KV * HEAD_DIM
    rope_spec = pl.BlockSpec((tq, HEAD_DIM), lambda b, g, i: (i, 0))
    in_specs = [pl.BlockSpec((tq, gw), lambda b, g, i: (b * nq + i, COL_Q // gw + g)),
                pl.BlockSpec((tq, gw), lambda b, g, i: (b * nq + i, COL_AG // gw + g)),
                pl.BlockSpec((keys, HEAD_DIM), lambda b, g, i: (b, g)),
                pl.BlockSpec((HEAD_DIM, keys), lambda b, g, i: (b * ATT_KV_HEADS + g, 0)),
                pl.BlockSpec((1,) + k_norm2.shape[1:], lambda b, g, i: (b, 0, 0)),
                pl.BlockSpec((1, HEAD_DIM), lambda b, g, i: (0, 0)),
                rope_spec, rope_spec]
    args = [proj, proj, k_all, vt_all, k_norm2, q_norm_w.reshape(1, HEAD_DIM), *rope_tables]
    return pl.pallas_call(
        functools.partial(_attn_kernel, tq=tq, tk=tk, n_chunks=keys // tk),
        grid=(batch, ATT_KV_HEADS, nq),
        in_specs=in_specs,
        out_specs=pl.BlockSpec((tq, gw), lambda b, g, i: (b * nq + i, g)),
        out_shape=jax.ShapeDtypeStruct((t, ATT_WIDTH), BF16),
        scratch_shapes=[pltpu.VMEM((HEAD_DIM, HEADS_PER_KV * tq), BF16),
                        pltpu.VMEM((tk, HEADS_PER_KV * tq), F32),
                        pltpu.VMEM((tk, HEADS_PER_KV * tq), F32),
                        pltpu.VMEM((1, HEADS_PER_KV * tq), F32),
                        pltpu.VMEM((1, HEADS_PER_KV * tq), F32),
                        pltpu.VMEM((HEAD_DIM, HEADS_PER_KV * tq), F32)],
        compiler_params=pltpu.CompilerParams(dimension_semantics=("parallel", "parallel", "arbitrary"),
                                             vmem_limit_bytes=VMEM_LIMIT),
        name="attention",
    )(*args)


def _ctx_attn_kernel(*refs):
    q_ref, kv_ref, gate0_ref, gate1_ref, qw_ref, kw_ref = refs[:6]
    o_ref, kn_ref, vf_ref = refs[-3:]
    seq = q_ref.shape[0]
    q_scale = HEAD_DIM ** -0.5 * LOG2_E
    gates = (gate0_ref, gate1_ref)
    keys, values_t = [], []
    for g in range(ATT_KV_HEADS):
        dims = slice(g * HEAD_DIM, (g + 1) * HEAD_DIM)
        kn = _rms(kv_ref[:, dims].astype(F32), kw_ref[...])
        kn_ref[pl.ds(g, seq, stride=ATT_KV_HEADS), :] = kn
        keys.append(kn.astype(BF16))
        v = kv_ref[:, KV_WIDTH + g * HEAD_DIM:KV_WIDTH + (g + 1) * HEAD_DIM].astype(F32)
        vf_ref[pl.ds(g, seq, stride=ATT_KV_HEADS), :] = v
        values_t.append(v.T.astype(BF16))

    pending = {}
    for idx in range(ATT_HEADS + ATT_LOOKAHEAD):
        if idx < ATT_HEADS:
            dims = slice(idx * HEAD_DIM, (idx + 1) * HEAD_DIM)
            y = _rms(q_ref[:, dims].astype(F32), qw_ref[...]) * q_scale
            pending[idx] = jnp.dot(keys[idx // HEADS_PER_KV], y.T.astype(BF16), preferred_element_type=F32)
        head = idx - ATT_LOOKAHEAD
        if head >= 0:
            g, h = divmod(head, HEADS_PER_KV)
            s = pending.pop(head)
            p = jnp.exp2(s - jnp.max(s, axis=0, keepdims=True))
            l = jnp.sum(p, axis=0, keepdims=True)
            acc = jnp.dot(values_t[g], p.astype(BF16), preferred_element_type=F32)
            gate = gates[g][:, h * HEAD_DIM:(h + 1) * HEAD_DIM].astype(F32)
            o_ref[:, head * HEAD_DIM:(head + 1) * HEAD_DIM] = ((acc / l).T * _silu(gate)).astype(BF16)


def _context_attention(proj, q_norm_w, k_norm_w, batch, seq_len, collected, layer, depth):
    t = proj.shape[0]
    gw = HEADS_PER_KV * HEAD_DIM
    row_block = lambda width, col: pl.BlockSpec((seq_len, width), lambda b: (b, col // width))
    vec = pl.BlockSpec((1, HEAD_DIM), lambda b: (0, 0))
    in_specs = [row_block(ATT_WIDTH, COL_Q), row_block(2 * KV_WIDTH, COL_K),
                row_block(gw, COL_AG), row_block(gw, COL_AG + gw), vec, vec]
    args = [proj, proj, proj, proj, q_norm_w.reshape(1, HEAD_DIM), k_norm_w.reshape(1, HEAD_DIM)]
    aliases = {}
    if collected is not None:
        aliases = {len(args): 1, len(args) + 1: 2}
        in_specs += [pl.BlockSpec(memory_space=pl.ANY)] * 2
        args += list(collected)
    rows = seq_len * ATT_KV_HEADS
    slot = pl.BlockSpec((rows, HEAD_DIM), lambda b: (b * depth + layer, 0))
    att, keys, values = pl.pallas_call(
        _ctx_attn_kernel,
        grid=(batch,),
        in_specs=in_specs,
        out_specs=[row_block(ATT_WIDTH, 0), slot, slot],
        out_shape=[jax.ShapeDtypeStruct((t, ATT_WIDTH), BF16),
                   jax.ShapeDtypeStruct((batch * depth * rows, HEAD_DIM), F32),
                   jax.ShapeDtypeStruct((batch * depth * rows, HEAD_DIM), F32)],
        input_output_aliases=aliases,
        compiler_params=pltpu.CompilerParams(dimension_semantics=("parallel",),
                                             vmem_limit_bytes=VMEM_LIMIT),
        name="context_attention",
    )(*args)
    return att, (keys, values)


def _log_sigmoid(x):
    return -(jnp.maximum(-x, 0.0) + jnp.log(1.0 + jnp.exp(-jnp.abs(x))))


def _gla_level_table():
    t = np.arange(GLA_CHUNK)[:, None]
    s = np.arange(GLA_CHUNK)[None, :]
    lvl = np.full((GLA_CHUNK, GLA_CHUNK), len(GLA_LEVELS), np.int32)
    for i, size in reversed(list(enumerate(GLA_LEVELS))):
        lvl = np.where(t // size == s // size, i, lvl)
    fwd = np.where(s <= t, lvl, len(GLA_LEVELS)).astype(np.int32)
    return jnp.asarray(np.stack([fwd, fwd.T]))


def _minus_rows(b, picks, size):
    return jnp.concatenate(
        [b[i * size:(i + 1) * size] if p is None else b[i * size:(i + 1) * size] - b[p:p + 1]
         for i, p in enumerate(picks)], axis=0)


def _gla_decays(lr, a, bias):
    logits = jnp.dot(lr, a, preferred_element_type=F32) + bias
    return _log_sigmoid(logits) * (1.0 / GLA_TAU)


def _gla_cumsum(g, lvl):
    g_hi = g.astype(BF16)
    g_lo = (g - g_hi.astype(F32)).astype(BF16)
    tri = jnp.where(lvl < len(GLA_LEVELS), 1.0, 0.0).astype(BF16)
    return jnp.dot(tri, g_hi, preferred_element_type=F32) + jnp.dot(tri, g_lo, preferred_element_type=F32)


def _gla_operands(qk, b, rev):
    c = GLA_CHUNK
    q = qk[:, :GLA_KW].astype(F32) * GLA_DK ** -0.5
    k = qk[:, GLA_KW:].astype(F32)
    b_total = b[0:1] if rev else b[c - 1:c]
    q_inter = (q * jnp.exp(b)).astype(BF16)
    k_state = (k * jnp.exp(b_total - b)).astype(BF16)
    decay_total = jnp.exp(b_total)

    row = lax.broadcasted_iota(jnp.int32, (c, GLA_KW), 0)
    pairs = []
    for size in GLA_LEVELS:
        n = c // size
        if size == GLA_LEVELS[0]:
            if rev:
                picks = [(i + 1) * size if i + 1 < n else None for i in range(n)]
            else:
                picks = [i * size - 1 if i > 0 else None for i in range(n)]
            d = _minus_rows(b, picks, size)
            q_hat = q * jnp.exp(d)
            k_hat = k * jnp.exp(jnp.minimum(-d, GLA_EXP_CLAMP))
        else:
            half = size // 2
            d = _minus_rows(b, [i * size + half - (0 if rev else 1) for i in range(n)], size)
            late = (row & half) != 0
            q_side = jnp.logical_not(late) if rev else late
            q_hat = q * jnp.exp(jnp.where(q_side, d, -jnp.inf))
            k_hat = k * jnp.exp(jnp.where(q_side, -jnp.inf, -d))
        pairs.append((q_hat.astype(BF16), k_hat.astype(BF16)))
    return q_inter, k_state, decay_total, pairs


def _gla_kernel(*refs):
    qkf_ref, vf_ref, lrf_ref, qkb_ref, vb_ref, lrb_ref, a_ref, bias_ref, lvl_ref, s0_ref = refs[:10]
    of_ref, ob_ref, so_ref, st_ref = refs[-4:]
    j = pl.program_id(1)

    @pl.when(j == 0)
    def _():
        for z in range(2):
            for h in range(GLA_HEADS):
                st_ref[z, h] = s0_ref[0, z, h].T

    n_chunks = qkf_ref.shape[0] // GLA_CHUNK
    chunk_rows = [slice(c * GLA_CHUNK, (c + 1) * GLA_CHUNK) for c in range(n_chunks)]
    units = [(z, rows) for rows_f, rows_b in zip(chunk_rows, reversed(chunk_rows))
             for z, rows in ((0, rows_f), (1, rows_b))]
    src = ((qkf_ref, vf_ref, lrf_ref, of_ref), (qkb_ref, vb_ref, lrb_ref, ob_ref))
    n_levels = len(GLA_LEVELS)
    hk = [slice(h * GLA_DK, (h + 1) * GLA_DK) for h in range(GLA_HEADS)]
    hv = [slice(h * GLA_DV, (h + 1) * GLA_DV) for h in range(GLA_HEADS)]
    tn = (((0,), (0,)), ((), ()))

    g = [_gla_decays(src[z][2][rows, :], a_ref[z], bias_ref[z]) for z, rows in units]
    b = [_gla_cumsum(gi, lvl_ref[z]) for gi, (z, rows) in zip(g, units)]
    v = [src[z][1][rows, :] for z, rows in units]
    state = [[st_ref[z, h] for h in range(GLA_HEADS)] for z in range(2)]
    ops, delta, scores = {}, {}, {}

    def queue_scores(u):
        z, rows = units[u]
        ops[u] = _gla_operands(src[z][0][rows, :], b[u], z == 1)
        delta[u] = [lax.dot_general(v[u][:, hv[h]], ops[u][1][:, hk[h]], tn, preferred_element_type=F32)
                    for h in range(GLA_HEADS)]
        scores[u] = [[lax.dot_general(qh[:, hk[h]], kh[:, hk[h]], _NT, preferred_element_type=F32)
                      for qh, kh in ops[u][3]] for h in range(GLA_HEADS)]

    def consume(u):
        z, rows = units[u]
        lvl = lvl_ref[z]
        q_inter, _, decay_total, _ = ops.pop(u)
        for h in range(GLA_HEADS):
            p = scores[u][h]
            att = p[-1]
            for i in reversed(range(n_levels - 1)):
                att = jnp.where(lvl == i, p[i], att)
            o = jnp.dot(att.astype(BF16), v[u][:, hv[h]], preferred_element_type=F32) + lax.dot_general(
                q_inter[:, hk[h]], state[z][h].astype(BF16), _NT, preferred_element_type=F32)
            src[z][3][rows, hv[h]] = o.astype(BF16)
            state[z][h] = state[z][h] * decay_total[:, hk[h]] + delta[u][h]

    for u in range(len(units)):
        queue_scores(u)
        if u > 0:
            consume(u - 1)
    consume(len(units) - 1)
    for z in range(2):
        for h in range(GLA_HEADS):
            st_ref[z, h] = state[z][h]

    @pl.when(j == pl.num_programs(1) - 1)
    def _():
        for z in range(2):
            for h in range(GLA_HEADS):
                so_ref[0, 0, z, h] = st_ref[z, h].T


def _gla(proj, lr, a_pad, bias, s0, batch, seq_len, states=None, layer=0, depth=1):
    t = proj.shape[0]
    blk = min(GLA_BLOCK, seq_len)
    nb = seq_len // blk
    fwd = lambda b, j: b * nb + j
    bwd = lambda b, j: b * nb + nb - 1 - j
    qk_w = 2 * GLA_KW

    def proj_specs(row):
        return [pl.BlockSpec((blk, qk_w), lambda b, j: (row(b, j), COL_GQK // qk_w)),
                pl.BlockSpec((blk, GLA_VW), lambda b, j: (row(b, j), COL_GV // GLA_VW)),
                pl.BlockSpec((blk, LR_BLOCK), lambda b, j: (row(b, j), 0))]

    state = (2, GLA_HEADS, GLA_DK, GLA_DV)
    in_specs = proj_specs(fwd) + proj_specs(bwd) + [
        pl.BlockSpec((2, LR_BLOCK, GLA_KW), lambda b, j: (0, 0, 0)),
        pl.BlockSpec((2, 1, GLA_KW), lambda b, j: (0, 0, 0)),
        pl.BlockSpec((2, GLA_CHUNK, GLA_CHUNK), lambda b, j: (0, 0, 0)),
        pl.BlockSpec((1,) + state, lambda b, j: (b, 0, 0, 0, 0))]
    args = [proj, proj, lr, proj, proj, lr, a_pad, bias, _gla_level_table(), s0]
    aliases = {}
    if states is not None:
        aliases = {len(args): 2}
        in_specs.append(pl.BlockSpec(memory_space=pl.ANY))
        args.append(states)
    return pl.pallas_call(
        _gla_kernel,
        grid=(batch, nb),
        in_specs=in_specs,
        out_specs=[pl.BlockSpec((blk, GLA_VW), lambda b, j: (fwd(b, j), 0)),
                   pl.BlockSpec((blk, GLA_VW), lambda b, j: (bwd(b, j), 0)),
                   pl.BlockSpec((1, 1) + state, lambda b, j: (b, layer, 0, 0, 0, 0))],
        out_shape=[jax.ShapeDtypeStruct((t, GLA_VW), BF16),
                   jax.ShapeDtypeStruct((t, GLA_VW), BF16),
                   jax.ShapeDtypeStruct((batch, depth) + state, F32)],
        scratch_shapes=[pltpu.VMEM((2, GLA_HEADS, GLA_DV, GLA_DK), F32)],
        input_output_aliases=aliases,
        compiler_params=pltpu.CompilerParams(dimension_semantics=("parallel", "arbitrary"),
                                             vmem_limit_bytes=VMEM_LIMIT),
        name="gla",
    )(*args)


OUT_ROW_CHUNK = 512
HALO = 16


def _out_kernel(*refs, tm, seq_len, final):
    (x_ref, a_ref, ch_ref, cb_ref, cc_ref, cg_ref, chp_ref, ccp_ref, chn_ref, ccn_ref,
     gg_ref, of_ref, ob_ref, cw_ref, gnw_ref, gate_ref, w_ref) = refs[:17]
    pos = 17
    if final:
        fnw_ref = refs[pos]
        pos += 1
    o_ref = refs[pos]

    def prod(c_ref, h_ref, r):
        return c_ref[r:r + 1, :].astype(F32) * h_ref[r:r + 1, :].astype(F32)

    tc = OUT_ROW_CHUNK

    for lo in range(0, tm, tc):
        rows = slice(lo, lo + tc)

        u = cc_ref[rows, :].astype(F32) * ch_ref[rows, :].astype(F32)
        u_before = prod(ccp_ref, chp_ref, HALO - 1) if lo == 0 else prod(cc_ref, ch_ref, lo - 1)
        u_after = prod(ccn_ref, chn_ref, 0) if lo + tc == tm else prod(cc_ref, ch_ref, lo + tc)
        row = lax.broadcasted_iota(jnp.int32, u.shape, 0)
        seq_pos = (pl.program_id(0) * tm + lo + row) & (seq_len - 1)
        u_prev = jnp.where(row == 0, u_before, pltpu.roll(u, 1, axis=0))
        u_prev = jnp.where(seq_pos == 0, 0.0, u_prev)
        u_next = jnp.where(row == tc - 1, u_after, pltpu.roll(u, tc - 1, axis=0))
        u_next = jnp.where(seq_pos == seq_len - 1, 0.0, u_next)
        conv = u_prev * cw_ref[0:1, :] + u * cw_ref[1:2, :] + u_next * cw_ref[2:3, :]
        out_b = cb_ref[rows, :].astype(F32) * conv * _silu(cg_ref[rows, :].astype(F32))
        parts = [out_b.astype(BF16)]

        o = of_ref[rows, :].astype(F32) + ob_ref[rows, :].astype(F32)
        gg = gg_ref[rows, :].astype(F32)
        for h in range(GLA_HEADS):
            cols = slice(h * GLA_DV, (h + 1) * GLA_DV)
            parts.append((_rms(o[:, cols], gnw_ref[...]) * _silu(gg[:, cols])).astype(BF16))

        r = jnp.dot(a_ref[rows, :], w_ref[0, :ATT_WIDTH, :], preferred_element_type=F32)
        r = r + jnp.dot(jnp.concatenate(parts, axis=1), w_ref[0, ATT_WIDTH:, :], preferred_element_type=F32)
        xn = x_ref[rows, :] + gate_ref[0] * r
        if final:
            xn = _rms(xn, fnw_ref[...])
        o_ref[rows, :] = xn


def _out_projection(x, att, proj, o_f, o_b, conv_w, gla_norm_w, mod3, w_out, layer, final_norm_w, mod_row,
                    seq_len):
    t, d = x.shape
    tm = 512
    cw = conv_w.shape[1]
    final = final_norm_w is not None
    per_halo = tm // HALO
    last_halo = t // HALO - 1

    def col(c, width):
        return lambda i: (i, c // width)

    prev = lambda c: (lambda i: (jnp.maximum(i * per_halo - 1, 0), c // cw))
    nxt = lambda c: (lambda i: (jnp.minimum((i + 1) * per_halo, last_halo), c // cw))
    in_specs = [pl.BlockSpec((tm, d), lambda i: (i, 0)),
                pl.BlockSpec((tm, ATT_WIDTH), lambda i: (i, 0)),
                pl.BlockSpec((tm, cw), col(COL_CH, cw)),
                pl.BlockSpec((tm, cw), col(COL_CB, cw)),
                pl.BlockSpec((tm, cw), col(COL_CC, cw)),
                pl.BlockSpec((tm, cw), col(COL_CG, cw)),
                pl.BlockSpec((HALO, cw), prev(COL_CH)),
                pl.BlockSpec((HALO, cw), prev(COL_CC)),
                pl.BlockSpec((HALO, cw), nxt(COL_CH)),
                pl.BlockSpec((HALO, cw), nxt(COL_CC)),
                pl.BlockSpec((tm, GLA_VW), col(COL_GG, GLA_VW)),
                pl.BlockSpec((tm, GLA_VW), lambda i: (i, 0)),
                pl.BlockSpec((tm, GLA_VW), lambda i: (i, 0)),
                pl.BlockSpec((3, cw), lambda i: (0, 0)),
                pl.BlockSpec((1, GLA_DV), lambda i: (0, 0)),
                pl.BlockSpec((1, 1, d), lambda i: (mod_row(i * tm), 0, 2)),
                pl.BlockSpec((1,) + w_out.shape[1:], lambda i: (layer, 0, 0))]
    args = [x, att, proj, proj, proj, proj, proj, proj, proj, proj, proj, o_f, o_b,
            conv_w, gla_norm_w.reshape(1, GLA_DV), mod3, w_out]
    if final:
        in_specs.append(pl.BlockSpec((1, d), lambda i: (0, 0)))
        args.append(final_norm_w.reshape(1, d))
    return pl.pallas_call(
        functools.partial(_out_kernel, tm=tm, seq_len=seq_len, final=final),
        grid=(t // tm,),
        in_specs=in_specs,
        out_specs=pl.BlockSpec((tm, d), lambda i: (i, 0)),
        out_shape=jax.ShapeDtypeStruct((t, d), F32),
        compiler_params=pltpu.CompilerParams(dimension_semantics=("parallel",),
                                             vmem_limit_bytes=VMEM_LIMIT),
        name="out_projection",
    )(*args)


def _rope_tables(seq_len):
    pos = jnp.arange(seq_len)
    row = (pos // GRID_W).astype(F32)
    col = (pos % GRID_W).astype(F32)
    n_freq = HEAD_DIM // 4
    inv = ROPE_THETA ** (-jnp.arange(n_freq, dtype=F32) / n_freq)
    ang_r = row[:, None] * inv[None, :]
    ang_c = col[:, None] * inv[None, :]
    cos = jnp.concatenate([jnp.cos(ang_r), jnp.cos(ang_r), jnp.cos(ang_c), jnp.cos(ang_c)], axis=-1)
    sin = jnp.concatenate([-jnp.sin(ang_r), jnp.sin(ang_r), -jnp.sin(ang_c), jnp.sin(ang_c)], axis=-1)
    return cos, sin


@jax.jit
def _forward(x_prompt, x_sample, cache_k, cache_v, state_gla, c, c_ctx, w_mod, b_mod, norm_w, w_in,
             q_norm_w, k_norm_w, conv_w, gla_a_up, gla_a_bias, gla_norm_w, w_out, final_norm_w):
    batch, seq, d = x_prompt.shape
    dec_batch, dec_seq, _ = x_sample.shape
    depth = w_in.shape[0]
    past = cache_k.shape[2]

    cond = jnp.zeros((8, d), F32).at[0].set(c_ctx).at[1:1 + dec_batch].set(c)
    mod3 = _modulation(cond, w_mod, b_mod).reshape(depth * 8, 1, 3 * d)
    rope_tables = _rope_tables(dec_seq)

    h = x_prompt.reshape(batch * seq, d)
    z = x_sample.reshape(dec_batch * dec_seq, d)
    zero_state = jnp.zeros((batch, 2, GLA_HEADS, GLA_DK, GLA_DV), F32)
    w_in_t = jnp.swapaxes(w_in, 1, 2).astype(BF16)
    w_lr_t = jnp.pad(w_in_t[:, COL_LR:, :], ((0, 0), (0, LR_BLOCK - (IN_WIDTH - COL_LR)), (0, 0)))
    w_out_b = w_out.astype(BF16)
    new_kv, new_state = None, None
    for l in range(depth):
        a_pad = jnp.zeros((2, LR_BLOCK, GLA_KW), F32)
        for zdir in range(2):
            a_pad = a_pad.at[zdir, zdir * GLA_RANK:(zdir + 1) * GLA_RANK].set(gla_a_up[l, zdir])
        a_pad = a_pad.astype(BF16)
        bias = gla_a_bias[l].reshape(2, 1, GLA_KW)
        final_w = final_norm_w if l == depth - 1 else None

        row_ctx = lambda tok, l=l: l * 8
        proj, lr = _in_projection(h, mod3, norm_w[l], w_in_t, w_lr_t, l, row_ctx)
        att, new_kv = _context_attention(proj, q_norm_w[l], k_norm_w[l], batch, seq, new_kv, l, depth)
        o_f, o_b, new_state = _gla(proj, lr, a_pad, bias, zero_state, batch, seq, new_state, l, depth)
        h = _out_projection(h, att, proj, o_f, o_b, conv_w[l], gla_norm_w[l], mod3, w_out_b, l, final_w,
                            row_ctx, seq)

        row_lat = lambda tok, l=l: l * 8 + 1 + tok // dec_seq
        proj, lr = _in_projection(z, mod3, norm_w[l], w_in_t, w_lr_t, l, row_lat)
        ctx_kv = (cache_k[:, l].reshape(dec_batch * past, KV_WIDTH),
                  cache_v[:, l].reshape(dec_batch * past, KV_WIDTH))
        k_all, vt_all, k_norm2 = _kv_prepare(proj, k_norm_w[l], rope_tables, ctx_kv, dec_batch, dec_seq)
        att = _attention(proj, k_all, vt_all, k_norm2, q_norm_w[l], rope_tables, dec_batch, dec_seq)
        o_f, o_b, _ = _gla(proj, lr, a_pad, bias, state_gla[:, l], dec_batch, dec_seq)
        z = _out_projection(z, att, proj, o_f, o_b, conv_w[l], gla_norm_w[l], mod3, w_out_b, l, final_w,
                            row_lat, dec_seq)

    kv_shape = (batch, depth, seq, ATT_KV_HEADS, HEAD_DIM)
    return (h.reshape(batch, seq, d), z.reshape(dec_batch, dec_seq, d),
            new_kv[0].reshape(kv_shape), new_kv[1].reshape(kv_shape), new_state)


def kernel(x_prompt, x_sample, cache_k, cache_v, state_gla, c, c_ctx, w_mod, b_mod, norm_w, w_in, q_norm_w,
           k_norm_w, conv_w, gla_a_up, gla_a_bias, gla_norm_w, w_out, final_norm_w):
    return _forward(x_prompt, x_sample, cache_k, cache_v, state_gla, c, c_ctx, w_mod, b_mod, norm_w, w_in,
                    q_norm_w, k_norm_w, conv_w, gla_a_up, gla_a_bias, gla_norm_w, w_out, final_norm_w)
```

```python
import functools

import numpy as np
import jax
import jax.numpy as jnp
from jax import lax
from jax.experimental import pallas as pl
from jax.experimental.pallas import tpu as pltpu

F32 = jnp.float32
BF16 = jnp.bfloat16

GRID_W = 64
HEAD_DIM = 128
ATT_HEADS = 8
ATT_KV_HEADS = 2
HEADS_PER_KV = ATT_HEADS // ATT_KV_HEADS
ATT_WIDTH = ATT_HEADS * HEAD_DIM
KV_WIDTH = ATT_KV_HEADS * HEAD_DIM
GLA_HEADS = 4
GLA_DK = 64
GLA_DV = 128
GLA_KW = GLA_HEADS * GLA_DK
GLA_VW = GLA_HEADS * GLA_DV
GLA_RANK = 16
GLA_TAU = 16.0
ROPE_THETA = 10000.0
LOG2_E = 1.4426950408889634
ATT_DIRECT_EXP_LIMIT = 48.0
ATT_LOOKAHEAD = 2
EPS = 1e-6

COL_Q = 0
COL_K = 1024
COL_V = 1280
COL_AG = 1536
COL_CH = 2560
COL_CB = 3072
COL_CC = 3584
COL_CG = 4096
COL_GQK = 4608
COL_GV = 5120
COL_GG = 5632
COL_LR = 6144
IN_WIDTH = 6176
IN_TILE = 2048
IN_ROW_CHUNKS = 4
IN_MAIN = COL_LR
LR_BLOCK = 128

GLA_CHUNK = 128
GLA_LEVELS = (16, 32, 64, 128)
GLA_BLOCK = 256
GLA_EXP_CLAMP = 60.0

VMEM_LIMIT = 56 * 1024 * 1024


def _silu(x):
    return x * jax.nn.sigmoid(x)


def _rms(x, w):
    return x * lax.rsqrt(jnp.mean(x * x, axis=-1, keepdims=True) + EPS) * w


def _rope(y, cos, sin_signed):
    lane = lax.broadcasted_iota(jnp.int32, y.shape, 1)
    partner = jnp.where((lane & 63) < 32, pltpu.roll(y, 96, axis=1), pltpu.roll(y, 32, axis=1))
    return y * cos + partner * sin_signed


def _mod_kernel(c_ref, w_ref, b_ref, o_ref):
    a = _silu(c_ref[...]).astype(BF16)
    o_ref[0] = jnp.dot(a, w_ref[0].astype(BF16), preferred_element_type=F32) + b_ref[0]


def _modulation(cond, w_mod, b_mod):
    depth, d, n = w_mod.shape
    tn = 768
    return pl.pallas_call(
        _mod_kernel,
        grid=(depth, n // tn),
        in_specs=[pl.BlockSpec((8, d), lambda l, j: (0, 0)),
                  pl.BlockSpec((1, d, tn), lambda l, j: (l, 0, j)),
                  pl.BlockSpec((1, 1, tn), lambda l, j: (l, 0, j))],
        out_specs=pl.BlockSpec((1, 8, tn), lambda l, j: (l, 0, j)),
        out_shape=jax.ShapeDtypeStruct((depth, 8, n), F32),
        compiler_params=pltpu.CompilerParams(dimension_semantics=("parallel", "parallel"),
                                             vmem_limit_bytes=VMEM_LIMIT),
        name="modulation",
    )(cond, w_mod, b_mod.reshape(depth, 1, n))


_NT = (((1,), (1,)), ((), ()))


def _in_kernel(x_ref, nw_ref, shift_ref, scale_ref, w_ref, wlr_ref, o_ref, lr_ref, xn_ref):
    def project(xn, w):
        return lax.dot_general(xn, w, _NT, preferred_element_type=F32).astype(BF16)

    @pl.when(pl.program_id(1) == 0)
    def _():
        gain = nw_ref[...] * (1.0 + scale_ref[0])
        per = x_ref.shape[0] // IN_ROW_CHUNKS
        for r in range(IN_ROW_CHUNKS):
            rows = slice(r * per, (r + 1) * per)
            x = x_ref[rows, :]
            inv = lax.rsqrt(jnp.mean(x * x, axis=-1, keepdims=True) + EPS)
            xn = (x * inv * gain + shift_ref[0]).astype(BF16)
            xn_ref[rows, :] = xn
            o_ref[rows, :] = project(xn, w_ref[0])
            lr_ref[rows, :] = project(xn, wlr_ref[0])

    @pl.when(pl.program_id(1) > 0)
    def _():
        o_ref[...] = project(xn_ref[...], w_ref[0])


def _in_projection(x, mod3, norm_w, w_in_t, w_lr_t, layer, mod_row):
    t, d = x.shape
    tm = 1024
    return pl.pallas_call(
        _in_kernel,
        grid=(t // tm, IN_MAIN // IN_TILE),
        in_specs=[pl.BlockSpec((tm, d), lambda i, j: (i, 0)),
                  pl.BlockSpec((1, d), lambda i, j: (0, 0)),
                  pl.BlockSpec((1, 1, d), lambda i, j: (mod_row(i * tm), 0, 0)),
                  pl.BlockSpec((1, 1, d), lambda i, j: (mod_row(i * tm), 0, 1)),
                  pl.BlockSpec((1, IN_TILE, d), lambda i, j: (layer, j, 0)),
                  pl.BlockSpec((1, LR_BLOCK, d), lambda i, j: (layer, 0, 0))],
        out_specs=[pl.BlockSpec((tm, IN_TILE), lambda i, j: (i, j)),
                   pl.BlockSpec((tm, LR_BLOCK), lambda i, j: (i, 0))],
        out_shape=[jax.ShapeDtypeStruct((t, IN_MAIN), BF16),
                   jax.ShapeDtypeStruct((t, LR_BLOCK), BF16)],
        scratch_shapes=[pltpu.VMEM((tm, d), BF16)],
        compiler_params=pltpu.CompilerParams(dimension_semantics=("parallel", "arbitrary"),
                                             vmem_limit_bytes=VMEM_LIMIT),
        name="in_projection",
    )(x, norm_w.reshape(1, d), mod3, mod3, w_in_t, w_lr_t)


def _kv_kernel(kv_ref, kw_ref, cos_ref, sin_ref, ck_ref, cv_ref, k_ref, vt_ref, kmax_ref, *, n_self):
    @pl.when(pl.program_id(1) == 0)
    def _():
        kmax_ref[...] = jnp.zeros(kmax_ref.shape, F32)

    def put_keys(g, kb):
        cols = slice(g * HEAD_DIM, (g + 1) * HEAD_DIM)
        k_ref[:, cols] = kb
        kf = kb.astype(F32)
        n2 = jnp.max(jnp.sum(kf * kf, axis=1, keepdims=True), axis=0, keepdims=True)
        kmax_ref[0, g:g + 1, :] = jnp.maximum(kmax_ref[0, g:g + 1, :], jnp.broadcast_to(n2, (1, HEAD_DIM)))

    @pl.when(pl.program_id(1) < n_self)
    def _():
        for g in range(ATT_KV_HEADS):
            cols = slice(g * HEAD_DIM, (g + 1) * HEAD_DIM)
            y = _rope(_rms(kv_ref[:, cols].astype(F32), kw_ref[...]), cos_ref[...], sin_ref[...])
            put_keys(g, y.astype(BF16))
        vt_ref[...] = kv_ref[:, KV_WIDTH:].astype(F32).T.astype(BF16)

    @pl.when(pl.program_id(1) >= n_self)
    def _():
        for g in range(ATT_KV_HEADS):
            put_keys(g, ck_ref[:, g * HEAD_DIM:(g + 1) * HEAD_DIM].astype(BF16))
        vt_ref[...] = cv_ref[...].T.astype(BF16)


def _kv_prepare(proj, k_norm_w, rope_tables, ctx_kv, batch, seq_len):
    tr = 512
    n_self = seq_len // tr
    n_ctx = ctx_kv[0].shape[0] // batch // tr
    n_all = n_self + n_ctx
    own = lambda j: jnp.minimum(j, n_self - 1)
    ctx_spec = pl.BlockSpec((tr, KV_WIDTH), lambda b, j: (b * n_ctx + jnp.maximum(j - n_self, 0), 0))
    rope_spec = pl.BlockSpec((tr, HEAD_DIM), lambda b, j: (own(j), 0))
    return pl.pallas_call(
        functools.partial(_kv_kernel, n_self=n_self),
        grid=(batch, n_all),
        in_specs=[pl.BlockSpec((tr, 2 * KV_WIDTH), lambda b, j: (b * n_self + own(j), COL_K // (2 * KV_WIDTH))),
                  pl.BlockSpec((1, HEAD_DIM), lambda b, j: (0, 0)),
                  rope_spec, rope_spec, ctx_spec, ctx_spec],
        out_specs=[pl.BlockSpec((tr, KV_WIDTH), lambda b, j: (b * n_all + j, 0)),
                   pl.BlockSpec((KV_WIDTH, tr), lambda b, j: (b, j)),
                   pl.BlockSpec((1, 8, HEAD_DIM), lambda b, j: (b, 0, 0))],
        out_shape=[jax.ShapeDtypeStruct((batch * n_all * tr, KV_WIDTH), BF16),
                   jax.ShapeDtypeStruct((batch * KV_WIDTH, n_all * tr), BF16),
                   jax.ShapeDtypeStruct((batch, 8, HEAD_DIM), F32)],
        compiler_params=pltpu.CompilerParams(dimension_semantics=("parallel", "arbitrary")),
        name="kv_prepare",
    )(proj, k_norm_w.reshape(1, HEAD_DIM), *rope_tables, *ctx_kv)


def _attn_kernel(q_ref, gate_ref, k_ref, vt_ref, kmax_ref, qw_ref, cos_ref, sin_ref,
                 o_ref, qt_ref, s0_ref, s1_ref, m_ref, l_ref, acc_ref, *, tq, tk, n_chunks):
    q_scale = HEAD_DIM ** -0.5 * LOG2_E
    heads = [slice(h * tq, (h + 1) * tq) for h in range(HEADS_PER_KV)]
    dims = [slice(h * HEAD_DIM, (h + 1) * HEAD_DIM) for h in range(HEADS_PER_KV)]

    def rows(c):
        off = c * tk
        return pl.ds(off if isinstance(off, int) else pl.multiple_of(off, tk), tk)

    def prepare(h):
        y = _rope(_rms(q_ref[:, dims[h]].astype(F32), qw_ref[...]), cos_ref[...], sin_ref[...])
        qt_ref[:, heads[h]] = (y * q_scale).T.astype(BF16)
        l_ref[:, heads[h]] = jnp.zeros((1, tq), F32)
        acc_ref[:, heads[h]] = jnp.zeros((HEAD_DIM, tq), F32)

    def finish(h):
        o = (acc_ref[:, heads[h]] / l_ref[:, heads[h]]).T
        o_ref[:, dims[h]] = (o * _silu(gate_ref[:, dims[h]].astype(F32))).astype(BF16)

    w = qw_ref[...]
    q_norm2 = HEAD_DIM * q_scale * q_scale * jnp.max(w * w)
    key_norm2 = jnp.max(kmax_ref[0, pl.ds(pl.program_id(1), 1), :])
    bounded = q_norm2 * key_norm2 < ATT_DIRECT_EXP_LIMIT ** 2

    @pl.when(bounded)
    def _():
        pieces = [(h, c) for h in range(HEADS_PER_KV) for c in range(n_chunks)]
        pending = {}
        for idx in range(len(pieces) + ATT_LOOKAHEAD):
            if idx < len(pieces):
                h, c = pieces[idx]
                if c == 0:
                    prepare(h)
                pending[idx] = jnp.dot(k_ref[rows(c), :], qt_ref[:, heads[h]], preferred_element_type=F32)
            if idx >= ATT_LOOKAHEAD:
                h, c = pieces[idx - ATT_LOOKAHEAD]
                p = jnp.exp2(pending.pop(idx - ATT_LOOKAHEAD))
                l_ref[:, heads[h]] += jnp.sum(p, axis=0, keepdims=True)
                acc_ref[:, heads[h]] += jnp.dot(vt_ref[:, rows(c)], p.astype(BF16), preferred_element_type=F32)
                if c == n_chunks - 1:
                    finish(h)

    @pl.when(jnp.logical_not(bounded))
    def _():
        for h in range(HEADS_PER_KV):
            prepare(h)
        m_ref[...] = jnp.full(m_ref.shape, -jnp.inf, F32)

        def scores(c, s_ref):
            s_ref[...] = jnp.dot(k_ref[rows(c), :], qt_ref[...], preferred_element_type=F32)

        def update(c, s_ref):
            s = s_ref[...]
            m_old = m_ref[...]
            m_new = jnp.maximum(m_old, jnp.max(s, axis=0, keepdims=True))
            p = jnp.exp2(s - m_new)
            alpha = jnp.exp2(m_old - m_new)
            l_ref[...] = alpha * l_ref[...] + jnp.sum(p, axis=0, keepdims=True)
            pv = jnp.dot(vt_ref[:, rows(c)], p.astype(BF16), preferred_element_type=F32)
            acc_ref[...] = alpha * acc_ref[...] + pv
            m_ref[...] = m_new

        scores(0, s0_ref)
        n_pairs = (n_chunks - 1) // 2

        def body(i, carry):
            c = 2 * i
            scores(c + 1, s1_ref)
            update(c, s0_ref)
            scores(c + 2, s0_ref)
            update(c + 1, s1_ref)
            return carry

        lax.fori_loop(0, n_pairs, body, 0)
        done = 2 * n_pairs
        if n_chunks - done == 2:
            scores(done + 1, s1_ref)
            update(done, s0_ref)
            update(done + 1, s1_ref)
        else:
            update(done, s0_ref)
        for h in range(HEADS_PER_KV):
            finish(h)


def _attention(proj, k_all, vt_all, k_norm2, q_norm_w, rope_tables, batch, seq_len):
    t = proj.shape[0]
    keys = k_all.shape[0] // batch
    tq = 512
    tk = min(512, keys)
    nq = seq_len // tq
    gw = HEADS_PER_KV * HEAD_DIM
    rope_spec = pl.BlockSpec((tq, HEAD_DIM), lambda b, g, i: (i, 0))
    in_specs = [pl.BlockSpec((tq, gw), lambda b, g, i: (b * nq + i, COL_Q // gw + g)),
                pl.BlockSpec((tq, gw), lambda b, g, i: (b * nq + i, COL_AG // gw + g)),
                pl.BlockSpec((keys, HEAD_DIM), lambda b, g, i: (b, g)),
                pl.BlockSpec((HEAD_DIM, keys), lambda b, g, i: (b * ATT_KV_HEADS + g, 0)),
                pl.BlockSpec((1,) + k_norm2.shape[1:], lambda b, g, i: (b, 0, 0)),
                pl.BlockSpec((1, HEAD_DIM), lambda b, g, i: (0, 0)),
                rope_spec, rope_spec]
    args = [proj, proj, k_all, vt_all, k_norm2, q_norm_w.reshape(1, HEAD_DIM), *rope_tables]
    return pl.pallas_call(
        functools.partial(_attn_kernel, tq=tq, tk=tk, n_chunks=keys // tk),
        grid=(batch, ATT_KV_HEADS, nq),
        in_specs=in_specs,
        out_specs=pl.BlockSpec((tq, gw), lambda b, g, i: (b * nq + i, g)),
        out_shape=jax.ShapeDtypeStruct((t, ATT_WIDTH), BF16),
        scratch_shapes=[pltpu.VMEM((HEAD_DIM, HEADS_PER_KV * tq), BF16),
                        pltpu.VMEM((tk, HEADS_PER_KV * tq), F32),
                        pltpu.VMEM((tk, HEADS_PER_KV * tq), F32),
                        pltpu.VMEM((1, HEADS_PER_KV * tq), F32),
                        pltpu.VMEM((1, HEADS_PER_KV * tq), F32),
                        pltpu.VMEM((HEAD_DIM, HEADS_PER_KV * tq), F32)],
        compiler_params=pltpu.CompilerParams(dimension_semantics=("parallel", "parallel", "arbitrary"),
                                             vmem_limit_bytes=VMEM_LIMIT),
        name="attention",
    )(*args)


def _ctx_attn_kernel(*refs):
    q_ref, kv_ref, gate0_ref, gate1_ref, qw_ref, kw_ref = refs[:6]
    o_ref, kn_ref, vf_ref = refs[-3:]
    seq = q_ref.shape[0]
    q_scale = HEAD_DIM ** -0.5 * LOG2_E
    gates = (gate0_ref, gate1_ref)
    keys, values_t = [], []
    for g in range(ATT_KV_HEADS):
        dims = slice(g * HEAD_DIM, (g + 1) * HEAD_DIM)
        kn = _rms(kv_ref[:, dims].astype(F32), kw_ref[...])
        kn_ref[pl.ds(g, seq, stride=ATT_KV_HEADS), :] = kn
        keys.append(kn.astype(BF16))
        v = kv_ref[:, KV_WIDTH + g * HEAD_DIM:KV_WIDTH + (g + 1) * HEAD_DIM].astype(F32)
        vf_ref[pl.ds(g, seq, stride=ATT_KV_HEADS), :] = v
        values_t.append(v.T.astype(BF16))

    pending = {}
    for idx in range(ATT_HEADS + ATT_LOOKAHEAD):
        if idx < ATT_HEADS:
            dims = slice(idx * HEAD_DIM, (idx + 1) * HEAD_DIM)
            y = _rms(q_ref[:, dims].astype(F32), qw_ref[...]) * q_scale
            pending[idx] = jnp.dot(keys[idx // HEADS_PER_KV], y.T.astype(BF16), preferred_element_type=F32)
        head = idx - ATT_LOOKAHEAD
        if head >= 0:
            g, h = divmod(head, HEADS_PER_KV)
            s = pending.pop(head)
            p = jnp.exp2(s - jnp.max(s, axis=0, keepdims=True))
            l = jnp.sum(p, axis=0, keepdims=True)
            acc = jnp.dot(values_t[g], p.astype(BF16), preferred_element_type=F32)
            gate = gates[g][:, h * HEAD_DIM:(h + 1) * HEAD_DIM].astype(F32)
            o_ref[:, head * HEAD_DIM:(head + 1) * HEAD_DIM] = ((acc / l).T * _silu(gate)).astype(BF16)


def _context_attention(proj, q_norm_w, k_norm_w, batch, seq_len, collected, layer, depth):
    t = proj.shape[0]
    gw = HEADS_PER_KV * HEAD_DIM
    row_block = lambda width, col: pl.BlockSpec((seq_len, width), lambda b: (b, col // width))
    vec = pl.BlockSpec((1, HEAD_DIM), lambda b: (0, 0))
    in_specs = [row_block(ATT_WIDTH, COL_Q), row_block(2 * KV_WIDTH, COL_K),
                row_block(gw, COL_AG), row_block(gw, COL_AG + gw), vec, vec]
    args = [proj, proj, proj, proj, q_norm_w.reshape(1, HEAD_DIM), k_norm_w.reshape(1, HEAD_DIM)]
    aliases = {}
    if collected is not None:
        aliases = {len(args): 1, len(args) + 1: 2}
        in_specs += [pl.BlockSpec(memory_space=pl.ANY)] * 2
        args += list(collected)
    rows = seq_len * ATT_KV_HEADS
    slot = pl.BlockSpec((rows, HEAD_DIM), lambda b: (b * depth + layer, 0))
    att, keys, values = pl.pallas_call(
        _ctx_attn_kernel,
        grid=(batch,),
        in_specs=in_specs,
        out_specs=[row_block(ATT_WIDTH, 0), slot, slot],
        out_shape=[jax.ShapeDtypeStruct((t, ATT_WIDTH), BF16),
                   jax.ShapeDtypeStruct((batch * depth * rows, HEAD_DIM), F32),
                   jax.ShapeDtypeStruct((batch * depth * rows, HEAD_DIM), F32)],
        input_output_aliases=aliases,
        compiler_params=pltpu.CompilerParams(dimension_semantics=("parallel",),
                                             vmem_limit_bytes=VMEM_LIMIT),
        name="context_attention",
    )(*args)
    return att, (keys, values)


def _log_sigmoid(x):
    return -(jnp.maximum(-x, 0.0) + jnp.log(1.0 + jnp.exp(-jnp.abs(x))))


def _gla_level_table():
    t = np.arange(GLA_CHUNK)[:, None]
    s = np.arange(GLA_CHUNK)[None, :]
    lvl = np.full((GLA_CHUNK, GLA_CHUNK), len(GLA_LEVELS), np.int32)
    for i, size in reversed(list(enumerate(GLA_LEVELS))):
        lvl = np.where(t // size == s // size, i, lvl)
    fwd = np.where(s <= t, lvl, len(GLA_LEVELS)).astype(np.int32)
    return jnp.asarray(np.stack([fwd, fwd.T]))


def _minus_rows(b, picks, size):
    return jnp.concatenate(
        [b[i * size:(i + 1) * size] if p is None else b[i * size:(i + 1) * size] - b[p:p + 1]
         for i, p in enumerate(picks)], axis=0)


def _gla_decays(lr, a, bias):
    logits = jnp.dot(lr, a, preferred_element_type=F32) + bias
    return _log_sigmoid(logits) * (1.0 / GLA_TAU)


def _gla_cumsum(g, lvl):
    g_hi = g.astype(BF16)
    g_lo = (g - g_hi.astype(F32)).astype(BF16)
    tri = jnp.where(lvl < len(GLA_LEVELS), 1.0, 0.0).astype(BF16)
    return jnp.dot(tri, g_hi, preferred_element_type=F32) + jnp.dot(tri, g_lo, preferred_element_type=F32)


def _gla_operands(qk, b, rev):
    c = GLA_CHUNK
    q = qk[:, :GLA_KW].astype(F32) * GLA_DK ** -0.5
    k = qk[:, GLA_KW:].astype(F32)
    b_total = b[0:1] if rev else b[c - 1:c]
    q_inter = (q * jnp.exp(b)).astype(BF16)
    k_state = (k * jnp.exp(b_total - b)).astype(BF16)
    decay_total = jnp.exp(b_total)

    row = lax.broadcasted_iota(jnp.int32, (c, GLA_KW), 0)
    pairs = []
    for size in GLA_LEVELS:
        n = c // size
        if size == GLA_LEVELS[0]:
            if rev:
                picks = [(i + 1) * size if i + 1 < n else None for i in range(n)]
            else:
                picks = [i * size - 1 if i > 0 else None for i in range(n)]
            d = _minus_rows(b, picks, size)
            q_hat = q * jnp.exp(d)
            k_hat = k * jnp.exp(jnp.minimum(-d, GLA_EXP_CLAMP))
        else:
            half = size // 2
            d = _minus_rows(b, [i * size + half - (0 if rev else 1) for i in range(n)], size)
            late = (row & half) != 0
            q_side = jnp.logical_not(late) if rev else late
            q_hat = q * jnp.exp(jnp.where(q_side, d, -jnp.inf))
            k_hat = k * jnp.exp(jnp.where(q_side, -jnp.inf, -d))
        pairs.append((q_hat.astype(BF16), k_hat.astype(BF16)))
    return q_inter, k_state, decay_total, pairs


def _gla_kernel(*refs):
    qkf_ref, vf_ref, lrf_ref, qkb_ref, vb_ref, lrb_ref, a_ref, bias_ref, lvl_ref, s0_ref = refs[:10]
    of_ref, ob_ref, so_ref, st_ref = refs[-4:]
    j = pl.program_id(1)

    @pl.when(j == 0)
    def _():
        for z in range(2):
            for h in range(GLA_HEADS):
                st_ref[z, h] = s0_ref[0, z, h].T

    n_chunks = qkf_ref.shape[0] // GLA_CHUNK
    chunk_rows = [slice(c * GLA_CHUNK, (c + 1) * GLA_CHUNK) for c in range(n_chunks)]
    units = [(z, rows) for rows_f, rows_b in zip(chunk_rows, reversed(chunk_rows))
             for z, rows in ((0, rows_f), (1, rows_b))]
    src = ((qkf_ref, vf_ref, lrf_ref, of_ref), (qkb_ref, vb_ref, lrb_ref, ob_ref))
    n_levels = len(GLA_LEVELS)
    hk = [slice(h * GLA_DK, (h + 1) * GLA_DK) for h in range(GLA_HEADS)]
    hv = [slice(h * GLA_DV, (h + 1) * GLA_DV) for h in range(GLA_HEADS)]
    tn = (((0,), (0,)), ((), ()))

    g = [_gla_decays(src[z][2][rows, :], a_ref[z], bias_ref[z]) for z, rows in units]
    b = [_gla_cumsum(gi, lvl_ref[z]) for gi, (z, rows) in zip(g, units)]
    v = [src[z][1][rows, :] for z, rows in units]
    state = [[st_ref[z, h] for h in range(GLA_HEADS)] for z in range(2)]
    ops, delta, scores = {}, {}, {}

    def queue_scores(u):
        z, rows = units[u]
        ops[u] = _gla_operands(src[z][0][rows, :], b[u], z == 1)
        delta[u] = [lax.dot_general(v[u][:, hv[h]], ops[u][1][:, hk[h]], tn, preferred_element_type=F32)
                    for h in range(GLA_HEADS)]
        scores[u] = [[lax.dot_general(qh[:, hk[h]], kh[:, hk[h]], _NT, preferred_element_type=F32)
                      for qh, kh in ops[u][3]] for h in range(GLA_HEADS)]

    def consume(u):
        z, rows = units[u]
        lvl = lvl_ref[z]
        q_inter, _, decay_total, _ = ops.pop(u)
        for h in range(GLA_HEADS):
            p = scores[u][h]
            att = p[-1]
            for i in reversed(range(n_levels - 1)):
                att = jnp.where(lvl == i, p[i], att)
            o = jnp.dot(att.astype(BF16), v[u][:, hv[h]], preferred_element_type=F32) + lax.dot_general(
                q_inter[:, hk[h]], state[z][h].astype(BF16), _NT, preferred_element_type=F32)
            src[z][3][rows, hv[h]] = o.astype(BF16)
            state[z][h] = state[z][h] * decay_total[:, hk[h]] + delta[u][h]

    for u in range(len(units)):
        queue_scores(u)
        if u > 0:
            consume(u - 1)
    consume(len(units) - 1)
    for z in range(2):
        for h in range(GLA_HEADS):
            st_ref[z, h] = state[z][h]

    @pl.when(j == pl.num_programs(1) - 1)
    def _():
        for z in range(2):
            for h in range(GLA_HEADS):
                so_ref[0, 0, z, h] = st_ref[z, h].T


def _gla(proj, lr, a_pad, bias, s0, batch, seq_len, states=None, layer=0, depth=1):
    t = proj.shape[0]
    blk = min(GLA_BLOCK, seq_len)
    nb = seq_len // blk
    fwd = lambda b, j: b * nb + j
    bwd = lambda b, j: b * nb + nb - 1 - j
    qk_w = 2 * GLA_KW

    def proj_specs(row):
        return [pl.BlockSpec((blk, qk_w), lambda b, j: (row(b, j), COL_GQK // qk_w)),
                pl.BlockSpec((blk, GLA_VW), lambda b, j: (row(b, j), COL_GV // GLA_VW)),
                pl.BlockSpec((blk, LR_BLOCK), lambda b, j: (row(b, j), 0))]

    state = (2, GLA_HEADS, GLA_DK, GLA_DV)
    in_specs = proj_specs(fwd) + proj_specs(bwd) + [
        pl.BlockSpec((2, LR_BLOCK, GLA_KW), lambda b, j: (0, 0, 0)),
        pl.BlockSpec((2, 1, GLA_KW), lambda b, j: (0, 0, 0)),
        pl.BlockSpec((2, GLA_CHUNK, GLA_CHUNK), lambda b, j: (0, 0, 0)),
        pl.BlockSpec((1,) + state, lambda b, j: (b, 0, 0, 0, 0))]
    args = [proj, proj, lr, proj, proj, lr, a_pad, bias, _gla_level_table(), s0]
    aliases = {}
    if states is not None:
        aliases = {len(args): 2}
        in_specs.append(pl.BlockSpec(memory_space=pl.ANY))
        args.append(states)
    return pl.pallas_call(
        _gla_kernel,
        grid=(batch, nb),
        in_specs=in_specs,
        out_specs=[pl.BlockSpec((blk, GLA_VW), lambda b, j: (fwd(b, j), 0)),
                   pl.BlockSpec((blk, GLA_VW), lambda b, j: (bwd(b, j), 0)),
                   pl.BlockSpec((1, 1) + state, lambda b, j: (b, layer, 0, 0, 0, 0))],
        out_shape=[jax.ShapeDtypeStruct((t, GLA_VW), BF16),
                   jax.ShapeDtypeStruct((t, GLA_VW), BF16),
                   jax.ShapeDtypeStruct((batch, depth) + state, F32)],
        scratch_shapes=[pltpu.VMEM((2, GLA_HEADS, GLA_DV, GLA_DK), F32)],
        input_output_aliases=aliases,
        compiler_params=pltpu.CompilerParams(dimension_semantics=("parallel", "arbitrary"),
                                             vmem_limit_bytes=VMEM_LIMIT),
        name="gla",
    )(*args)


OUT_ROW_CHUNK = 512
HALO = 16


def _out_kernel(*refs, tm, seq_len, final):
    (x_ref, a_ref, ch_ref, cb_ref, cc_ref, cg_ref, chp_ref, ccp_ref, chn_ref, ccn_ref,
     gg_ref, of_ref, ob_ref, cw_ref, gnw_ref, gate_ref, w_ref) = refs[:17]
    pos = 17
    if final:
        fnw_ref = refs[pos]
        pos += 1
    o_ref = refs[pos]

    def prod(c_ref, h_ref, r):
        return c_ref[r:r + 1, :].astype(F32) * h_ref[r:r + 1, :].astype(F32)

    tc = OUT_ROW_CHUNK

    for lo in range(0, tm, tc):
        rows = slice(lo, lo + tc)

        u = cc_ref[rows, :].astype(F32) * ch_ref[rows, :].astype(F32)
        u_before = prod(ccp_ref, chp_ref, HALO - 1) if lo == 0 else prod(cc_ref, ch_ref, lo - 1)
        u_after = prod(ccn_ref, chn_ref, 0) if lo + tc == tm else prod(cc_ref, ch_ref, lo + tc)
        row = lax.broadcasted_iota(jnp.int32, u.shape, 0)
        seq_pos = (pl.program_id(0) * tm + lo + row) & (seq_len - 1)
        u_prev = jnp.where(row == 0, u_before, pltpu.roll(u, 1, axis=0))
        u_prev = jnp.where(seq_pos == 0, 0.0, u_prev)
        u_next = jnp.where(row == tc - 1, u_after, pltpu.roll(u, tc - 1, axis=0))
        u_next = jnp.where(seq_pos == seq_len - 1, 0.0, u_next)
        conv = u_prev * cw_ref[0:1, :] + u * cw_ref[1:2, :] + u_next * cw_ref[2:3, :]
        out_b = cb_ref[rows, :].astype(F32) * conv * _silu(cg_ref[rows, :].astype(F32))
        parts = [out_b.astype(BF16)]

        o = of_ref[rows, :].astype(F32) + ob_ref[rows, :].astype(F32)
        gg = gg_ref[rows, :].astype(F32)
        for h in range(GLA_HEADS):
            cols = slice(h * GLA_DV, (h + 1) * GLA_DV)
            parts.append((_rms(o[:, cols], gnw_ref[...]) * _silu(gg[:, cols])).astype(BF16))

        r = jnp.dot(a_ref[rows, :], w_ref[0, :ATT_WIDTH, :], preferred_element_type=F32)
        r = r + jnp.dot(jnp.concatenate(parts, axis=1), w_ref[0, ATT_WIDTH:, :], preferred_element_type=F32)
        xn = x_ref[rows, :] + gate_ref[0] * r
        if final:
            xn = _rms(xn, fnw_ref[...])
        o_ref[rows, :] = xn


def _out_projection(x, att, proj, o_f, o_b, conv_w, gla_norm_w, mod3, w_out, layer, final_norm_w, mod_row,
                    seq_len):
    t, d = x.shape
    tm = 512
    cw = conv_w.shape[1]
    final = final_norm_w is not None
    per_halo = tm // HALO
    last_halo = t // HALO - 1

    def col(c, width):
        return lambda i: (i, c // width)

    prev = lambda c: (lambda i: (jnp.maximum(i * per_halo - 1, 0), c // cw))
    nxt = lambda c: (lambda i: (jnp.minimum((i + 1) * per_halo, last_halo), c // cw))
    in_specs = [pl.BlockSpec((tm, d), lambda i: (i, 0)),
                pl.BlockSpec((tm, ATT_WIDTH), lambda i: (i, 0)),
                pl.BlockSpec((tm, cw), col(COL_CH, cw)),
                pl.BlockSpec((tm, cw), col(COL_CB, cw)),
                pl.BlockSpec((tm, cw), col(COL_CC, cw)),
                pl.BlockSpec((tm, cw), col(COL_CG, cw)),
                pl.BlockSpec((HALO, cw), prev(COL_CH)),
                pl.BlockSpec((HALO, cw), prev(COL_CC)),
                pl.BlockSpec((HALO, cw), nxt(COL_CH)),
                pl.BlockSpec((HALO, cw), nxt(COL_CC)),
                pl.BlockSpec((tm, GLA_VW), col(COL_GG, GLA_VW)),
                pl.BlockSpec((tm, GLA_VW), lambda i: (i, 0)),
                pl.BlockSpec((tm, GLA_VW), lambda i: (i, 0)),
                pl.BlockSpec((3, cw), lambda i: (0, 0)),
                pl.BlockSpec((1, GLA_DV), lambda i: (0, 0)),
                pl.BlockSpec((1, 1, d), lambda i: (mod_row(i * tm), 0, 2)),
                pl.BlockSpec((1,) + w_out.shape[1:], lambda i: (layer, 0, 0))]
    args = [x, att, proj, proj, proj, proj, proj, proj, proj, proj, proj, o_f, o_b,
            conv_w, gla_norm_w.reshape(1, GLA_DV), mod3, w_out]
    if final:
        in_specs.append(pl.BlockSpec((1, d), lambda i: (0, 0)))
        args.append(final_norm_w.reshape(1, d))
    return pl.pallas_call(
        functools.partial(_out_kernel, tm=tm, seq_len=seq_len, final=final),
        grid=(t // tm,),
        in_specs=in_specs,
        out_specs=pl.BlockSpec((tm, d), lambda i: (i, 0)),
        out_shape=jax.ShapeDtypeStruct((t, d), F32),
        compiler_params=pltpu.CompilerParams(dimension_semantics=("parallel",),
                                             vmem_limit_bytes=VMEM_LIMIT),
        name="out_projection",
    )(*args)


def _rope_tables(seq_len):
    pos = jnp.arange(seq_len)
    row = (pos // GRID_W).astype(F32)
    col = (pos % GRID_W).astype(F32)
    n_freq = HEAD_DIM // 4
    inv = ROPE_THETA ** (-jnp.arange(n_freq, dtype=F32) / n_freq)
    ang_r = row[:, None] * inv[None, :]
    ang_c = col[:, None] * inv[None, :]
    cos = jnp.concatenate([jnp.cos(ang_r), jnp.cos(ang_r), jnp.cos(ang_c), jnp.cos(ang_c)], axis=-1)
    sin = jnp.concatenate([-jnp.sin(ang_r), jnp.sin(ang_r), -jnp.sin(ang_c), jnp.sin(ang_c)], axis=-1)
    return cos, sin


@jax.jit
def _forward(x_prompt, x_sample, cache_k, cache_v, state_gla, c, c_ctx, w_mod, b_mod, norm_w, w_in,
             q_norm_w, k_norm_w, conv_w, gla_a_up, gla_a_bias, gla_norm_w, w_out, final_norm_w):
    batch, seq, d = x_prompt.shape
    dec_batch, dec_seq, _ = x_sample.shape
    depth = w_in.shape[0]
    past = cache_k.shape[2]

    cond = jnp.zeros((8, d), F32).at[0].set(c_ctx).at[1:1 + dec_batch].set(c)
    mod3 = _modulation(cond, w_mod, b_mod).reshape(depth * 8, 1, 3 * d)
    rope_tables = _rope_tables(dec_seq)

    h = x_prompt.reshape(batch * seq, d)
    z = x_sample.reshape(dec_batch * dec_seq, d)
    zero_state = jnp.zeros((batch, 2, GLA_HEADS, GLA_DK, GLA_DV), F32)
    w_in_t = jnp.swapaxes(w_in, 1, 2).astype(BF16)
    w_lr_t = jnp.pad(w_in_t[:, COL_LR:, :], ((0, 0), (0, LR_BLOCK - (IN_WIDTH - COL_LR)), (0, 0)))
    w_out_b = w_out.astype(BF16)
    new_kv, new_state = None, None
    for l in range(depth):
        a_pad = jnp.zeros((2, LR_BLOCK, GLA_KW), F32)
        for zdir in range(2):
            a_pad = a_pad.at[zdir, zdir * GLA_RANK:(zdir + 1) * GLA_RANK].set(gla_a_up[l, zdir])
        a_pad = a_pad.astype(BF16)
        bias = gla_a_bias[l].reshape(2, 1, GLA_KW)
        final_w = final_norm_w if l == depth - 1 else None

        row_ctx = lambda tok, l=l: l * 8
        proj, lr = _in_projection(h, mod3, norm_w[l], w_in_t, w_lr_t, l, row_ctx)
        att, new_kv = _context_attention(proj, q_norm_w[l], k_norm_w[l], batch, seq, new_kv, l, depth)
        o_f, o_b, new_state = _gla(proj, lr, a_pad, bias, zero_state, batch, seq, new_state, l, depth)
        h = _out_projection(h, att, proj, o_f, o_b, conv_w[l], gla_norm_w[l], mod3, w_out_b, l, final_w,
                            row_ctx, seq)

        row_lat = lambda tok, l=l: l * 8 + 1 + tok // dec_seq
        proj, lr = _in_projection(z, mod3, norm_w[l], w_in_t, w_lr_t, l, row_lat)
        ctx_kv = (cache_k[:, l].reshape(dec_batch * past, KV_WIDTH),
                  cache_v[:, l].reshape(dec_batch * past, KV_WIDTH))
        k_all, vt_all, k_norm2 = _kv_prepare(proj, k_norm_w[l], rope_tables, ctx_kv, dec_batch, dec_seq)
        att = _attention(proj, k_all, vt_all, k_norm2, q_norm_w[l], rope_tables, dec_batch, dec_seq)
        o_f, o_b, _ = _gla(proj, lr, a_pad, bias, state_gla[:, l], dec_batch, dec_seq)
        z = _out_projection(z, att, proj, o_f, o_b, conv_w[l], gla_norm_w[l], mod3, w_out_b, l, final_w,
                            row_lat, dec_seq)

    kv_shape = (batch, depth, seq, ATT_KV_HEADS, HEAD_DIM)
    return (h.reshape(batch, seq, d), z.reshape(dec_batch, dec_seq, d),
            new_kv[0].reshape(kv_shape), new_kv[1].reshape(kv_shape), new_state)


def kernel(x_prompt, x_sample, cache_k, cache_v, state_gla, c, c_ctx, w_mod, b_mod, norm_w, w_in, q_norm_w,
           k_norm_w, conv_w, gla_a_up, gla_a_bias, gla_norm_w, w_out, final_norm_w):
    return _forward(x_prompt, x_sample, cache_k, cache_v, state_gla, c, c_ctx, w_mod, b_mod, norm_w, w_in,
                    q_norm_w, k_norm_w, conv_w, gla_a_up, gla_a_bias, gla_norm_w, w_out, final_norm_w)
```

```python
import functools

import numpy as np
import jax
import jax.numpy as jnp
from jax import lax
from jax.experimental import pallas as pl
from jax.experimental.pallas import tpu as pltpu

F32 = jnp.float32
BF16 = jnp.bfloat16

GRID_W = 64
HEAD_DIM = 128
ATT_HEADS = 8
ATT_KV_HEADS = 2
HEADS_PER_KV = ATT_HEADS // ATT_KV_HEADS
ATT_WIDTH = ATT_HEADS * HEAD_DIM
KV_WIDTH = ATT_KV_HEADS * HEAD_DIM
GLA_HEADS = 4
GLA_DK = 64
GLA_DV = 128
GLA_KW = GLA_HEADS * GLA_DK
GLA_VW = GLA_HEADS * GLA_DV
GLA_RANK = 16
GLA_TAU = 16.0
ROPE_THETA = 10000.0
LOG2_E = 1.4426950408889634
ATT_DIRECT_EXP_LIMIT = 48.0
ATT_LOOKAHEAD = 2
EPS = 1e-6

COL_Q = 0
COL_K = 1024
COL_V = 1280
COL_AG = 1536
COL_CH = 2560
COL_CB = 3072
COL_CC = 3584
COL_CG = 4096
COL_GQK = 4608
COL_GV = 5120
COL_GG = 5632
COL_LR = 6144
IN_WIDTH = 6176
IN_TILE = 2048
IN_ROW_CHUNKS = 4
IN_MAIN = COL_LR
LR_BLOCK = 128

GLA_CHUNK = 128
GLA_LEVELS = (16, 32, 64, 128)
GLA_BLOCK = 256
GLA_EXP_CLAMP = 60.0

VMEM_LIMIT = 56 * 1024 * 1024


def _silu(x):
    return x * jax.nn.sigmoid(x)


def _rms(x, w):
    return x * lax.rsqrt(jnp.mean(x * x, axis=-1, keepdims=True) + EPS) * w


def _rope(y, cos, sin_signed):
    lane = lax.broadcasted_iota(jnp.int32, y.shape, 1)
    partner = jnp.where((lane & 63) < 32, pltpu.roll(y, 96, axis=1), pltpu.roll(y, 32, axis=1))
    return y * cos + partner * sin_signed


def _mod_kernel(c_ref, w_ref, b_ref, o_ref):
    a = _silu(c_ref[...]).astype(BF16)
    o_ref[0] = jnp.dot(a, w_ref[0].astype(BF16), preferred_element_type=F32) + b_ref[0]


def _modulation(cond, w_mod, b_mod):
    depth, d, n = w_mod.shape
    tn = 768
    return pl.pallas_call(
        _mod_kernel,
        grid=(depth, n // tn),
        in_specs=[pl.BlockSpec((8, d), lambda l, j: (0, 0)),
                  pl.BlockSpec((1, d, tn), lambda l, j: (l, 0, j)),
                  pl.BlockSpec((1, 1, tn), lambda l, j: (l, 0, j))],
        out_specs=pl.BlockSpec((1, 8, tn), lambda l, j: (l, 0, j)),
        out_shape=jax.ShapeDtypeStruct((depth, 8, n), F32),
        compiler_params=pltpu.CompilerParams(dimension_semantics=("parallel", "parallel"),
                                             vmem_limit_bytes=VMEM_LIMIT),
        name="modulation",
    )(cond, w_mod, b_mod.reshape(depth, 1, n))


_NT = (((1,), (1,)), ((), ()))


def _in_kernel(x_ref, nw_ref, shift_ref, scale_ref, w_ref, wlr_ref, o_ref, lr_ref, xn_ref):
    def project(xn, w):
        return lax.dot_general(xn, w, _NT, preferred_element_type=F32).astype(BF16)

    @pl.when(pl.program_id(1) == 0)
    def _():
        gain = nw_ref[...] * (1.0 + scale_ref[0])
        per = x_ref.shape[0] // IN_ROW_CHUNKS
        for r in range(IN_ROW_CHUNKS):
            rows = slice(r * per, (r + 1) * per)
            x = x_ref[rows, :]
            inv = lax.rsqrt(jnp.mean(x * x, axis=-1, keepdims=True) + EPS)
            xn = (x * inv * gain + shift_ref[0]).astype(BF16)
            xn_ref[rows, :] = xn
            o_ref[rows, :] = project(xn, w_ref[0])
            lr_ref[rows, :] = project(xn, wlr_ref[0])

    @pl.when(pl.program_id(1) > 0)
    def _():
        o_ref[...] = project(xn_ref[...], w_ref[0])


def _in_projection(x, mod3, norm_w, w_in_t, w_lr_t, layer, mod_row):
    t, d = x.shape
    tm = 1024
    return pl.pallas_call(
        _in_kernel,
        grid=(t // tm, IN_MAIN // IN_TILE),
        in_specs=[pl.BlockSpec((tm, d), lambda i, j: (i, 0)),
                  pl.BlockSpec((1, d), lambda i, j: (0, 0)),
                  pl.BlockSpec((1, 1, d), lambda i, j: (mod_row(i * tm), 0, 0)),
                  pl.BlockSpec((1, 1, d), lambda i, j: (mod_row(i * tm), 0, 1)),
                  pl.BlockSpec((1, IN_TILE, d), lambda i, j: (layer, j, 0)),
                  pl.BlockSpec((1, LR_BLOCK, d), lambda i, j: (layer, 0, 0))],
        out_specs=[pl.BlockSpec((tm, IN_TILE), lambda i, j: (i, j)),
                   pl.BlockSpec((tm, LR_BLOCK), lambda i, j: (i, 0))],
        out_shape=[jax.ShapeDtypeStruct((t, IN_MAIN), BF16),
                   jax.ShapeDtypeStruct((t, LR_BLOCK), BF16)],
        scratch_shapes=[pltpu.VMEM((tm, d), BF16)],
        compiler_params=pltpu.CompilerParams(dimension_semantics=("parallel", "arbitrary"),
                                             vmem_limit_bytes=VMEM_LIMIT),
        name="in_projection",
    )(x, norm_w.reshape(1, d), mod3, mod3, w_in_t, w_lr_t)


def _kv_kernel(kv_ref, kw_ref, cos_ref, sin_ref, ck_ref, cv_ref, k_ref, vt_ref, kmax_ref, *, n_self):
    @pl.when(pl.program_id(1) == 0)
    def _():
        kmax_ref[...] = jnp.zeros(kmax_ref.shape, F32)

    def put_keys(g, kb):
        cols = slice(g * HEAD_DIM, (g + 1) * HEAD_DIM)
        k_ref[:, cols] = kb
        kf = kb.astype(F32)
        n2 = jnp.max(jnp.sum(kf * kf, axis=1, keepdims=True), axis=0, keepdims=True)
        kmax_ref[0, g:g + 1, :] = jnp.maximum(kmax_ref[0, g:g + 1, :], jnp.broadcast_to(n2, (1, HEAD_DIM)))

    @pl.when(pl.program_id(1) < n_self)
    def _():
        for g in range(ATT_KV_HEADS):
            cols = slice(g * HEAD_DIM, (g + 1) * HEAD_DIM)
            y = _rope(_rms(kv_ref[:, cols].astype(F32), kw_ref[...]), cos_ref[...], sin_ref[...])
            put_keys(g, y.astype(BF16))
        vt_ref[...] = kv_ref[:, KV_WIDTH:].astype(F32).T.astype(BF16)

    @pl.when(pl.program_id(1) >= n_self)
    def _():
        for g in range(ATT_KV_HEADS):
            put_keys(g, ck_ref[:, g * HEAD_DIM:(g + 1) * HEAD_DIM].astype(BF16))
        vt_ref[...] = cv_ref[...].T.astype(BF16)


def _kv_prepare(proj, k_norm_w, rope_tables, ctx_kv, batch, seq_len):
    tr = 512
    n_self = seq_len // tr
    n_ctx = ctx_kv[0].shape[0] // batch // tr
    n_all = n_self + n_ctx
    own = lambda j: jnp.minimum(j, n_self - 1)
    ctx_spec = pl.BlockSpec((tr, KV_WIDTH), lambda b, j: (b * n_ctx + jnp.maximum(j - n_self, 0), 0))
    rope_spec = pl.BlockSpec((tr, HEAD_DIM), lambda b, j: (own(j), 0))
    return pl.pallas_call(
        functools.partial(_kv_kernel, n_self=n_self),
        grid=(batch, n_all),
        in_specs=[pl.BlockSpec((tr, 2 * KV_WIDTH), lambda b, j: (b * n_self + own(j), COL_K // (2 * KV_WIDTH))),
                  pl.BlockSpec((1, HEAD_DIM), lambda b, j: (0, 0)),
                  rope_spec, rope_spec, ctx_spec, ctx_spec],
        out_specs=[pl.BlockSpec((tr, KV_WIDTH), lambda b, j: (b * n_all + j, 0)),
                   pl.BlockSpec((KV_WIDTH, tr), lambda b, j: (b, j)),
                   pl.BlockSpec((1, 8, HEAD_DIM), lambda b, j: (b, 0, 0))],
        out_shape=[jax.ShapeDtypeStruct((batch * n_all * tr, KV_WIDTH), BF16),
                   jax.ShapeDtypeStruct((batch * KV_WIDTH, n_all * tr), BF16),
                   jax.ShapeDtypeStruct((batch, 8, HEAD_DIM), F32)],
        compiler_params=pltpu.CompilerParams(dimension_semantics=("parallel", "arbitrary")),
        name="kv_prepare",
    )(proj, k_norm_w.reshape(1, HEAD_DIM), *rope_tables, *ctx_kv)


def _attn_kernel(q_ref, gate_ref, k_ref, vt_ref, kmax_ref, qw_ref, cos_ref, sin_ref,
                 o_ref, qt_ref, s0_ref, s1_ref, m_ref, l_ref, acc_ref, *, tq, tk, n_chunks):
    q_scale = HEAD_DIM ** -0.5 * LOG2_E
    heads = [slice(h * tq, (h + 1) * tq) for h in range(HEADS_PER_KV)]
    dims = [slice(h * HEAD_DIM, (h + 1) * HEAD_DIM) for h in range(HEADS_PER_KV)]

    def rows(c):
        off = c * tk
        return pl.ds(off if isinstance(off, int) else pl.multiple_of(off, tk), tk)

    def prepare(h):
        y = _rope(_rms(q_ref[:, dims[h]].astype(F32), qw_ref[...]), cos_ref[...], sin_ref[...])
        qt_ref[:, heads[h]] = (y * q_scale).T.astype(BF16)
        l_ref[:, heads[h]] = jnp.zeros((1, tq), F32)
        acc_ref[:, heads[h]] = jnp.zeros((HEAD_DIM, tq), F32)

    def finish(h):
        o = (acc_ref[:, heads[h]] / l_ref[:, heads[h]]).T
        o_ref[:, dims[h]] = (o * _silu(gate_ref[:, dims[h]].astype(F32))).astype(BF16)

    w = qw_ref[...]
    q_norm2 = HEAD_DIM * q_scale * q_scale * jnp.max(w * w)
    key_norm2 = jnp.max(kmax_ref[0, pl.ds(pl.program_id(1), 1), :])
    bounded = q_norm2 * key_norm2 < ATT_DIRECT_EXP_LIMIT ** 2

    @pl.when(bounded)
    def _():
        pieces = [(h, c) for h in range(HEADS_PER_KV) for c in range(n_chunks)]
        pending = {}
        for idx in range(len(pieces) + ATT_LOOKAHEAD):
            if idx < len(pieces):
                h, c = pieces[idx]
                if c == 0:
                    prepare(h)
                pending[idx] = jnp.dot(k_ref[rows(c), :], qt_ref[:, heads[h]], preferred_element_type=F32)
            if idx >= ATT_LOOKAHEAD:
                h, c = pieces[idx - ATT_LOOKAHEAD]
                p = jnp.exp2(pending.pop(idx - ATT_LOOKAHEAD))
                l_ref[:, heads[h]] += jnp.sum(p, axis=0, keepdims=True)
                acc_ref[:, heads[h]] += jnp.dot(vt_ref[:, rows(c)], p.astype(BF16), preferred_element_type=F32)
                if c == n_chunks - 1:
                    finish(h)

    @pl.when(jnp.logical_not(bounded))
    def _():
        for h in range(HEADS_PER_KV):
            prepare(h)
        m_ref[...] = jnp.full(m_ref.shape, -jnp.inf, F32)

        def scores(c, s_ref):
            s_ref[...] = jnp.dot(k_ref[rows(c), :], qt_ref[...], preferred_element_type=F32)

        def update(c, s_ref):
            s = s_ref[...]
            m_old = m_ref[...]
            m_new = jnp.maximum(m_old, jnp.max(s, axis=0, keepdims=True))
            p = jnp.exp2(s - m_new)
            alpha = jnp.exp2(m_old - m_new)
            l_ref[...] = alpha * l_ref[...] + jnp.sum(p, axis=0, keepdims=True)
            pv = jnp.dot(vt_ref[:, rows(c)], p.astype(BF16), preferred_element_type=F32)
            acc_ref[...] = alpha * acc_ref[...] + pv
            m_ref[...] = m_new

        scores(0, s0_ref)
        n_pairs = (n_chunks - 1) // 2

        def body(i, carry):
            c = 2 * i
            scores(c + 1, s1_ref)
            update(c, s0_ref)
            scores(c + 2, s0_ref)
            update(c + 1, s1_ref)
            return carry

        lax.fori_loop(0, n_pairs, body, 0)
        done = 2 * n_pairs
        if n_chunks - done == 2:
            scores(done + 1, s1_ref)
            update(done, s0_ref)
            update(done + 1, s1_ref)
        else:
            update(done, s0_ref)
        for h in range(HEADS_PER_KV):
            finish(h)


def _attention(proj, k_all, vt_all, k_norm2, q_norm_w, rope_tables, batch, seq_len):
    t = proj.shape[0]
    keys = k_all.shape[0] // batch
    tq = 512
    tk = min(512, keys)
    nq = seq_len // tq
    gw = HEADS_PER_KV * HEAD_DIM
    rope_spec = pl.BlockSpec((tq, HEAD_DIM), lambda b, g, i: (i, 0))
    in_specs = [pl.BlockSpec((tq, gw), lambda b, g, i: (b * nq + i, COL_Q // gw + g)),
                pl.BlockSpec((tq, gw), lambda b, g, i: (b * nq + i, COL_AG // gw + g)),
                pl.BlockSpec((keys, HEAD_DIM), lambda b, g, i: (b, g)),
                pl.BlockSpec((HEAD_DIM, keys), lambda b, g, i: (b * ATT_KV_HEADS + g, 0)),
                pl.BlockSpec((1,) + k_norm2.shape[1:], lambda b, g, i: (b, 0, 0)),
                pl.BlockSpec((1, HEAD_DIM), lambda b, g, i: (0, 0)),
                rope_spec, rope_spec]
    args = [proj, proj, k_all, vt_all, k_norm2, q_norm_w.reshape(1, HEAD_DIM), *rope_tables]
    return pl.pallas_call(
        functools.partial(_attn_kernel, tq=tq, tk=tk, n_chunks=keys // tk),
        grid=(batch, ATT_KV_HEADS, nq),
        in_specs=in_specs,
        out_specs=pl.BlockSpec((tq, gw), lambda b, g, i: (b * nq + i, g)),
        out_shape=jax.ShapeDtypeStruct((t, ATT_WIDTH), BF16),
        scratch_shapes=[pltpu.VMEM((HEAD_DIM, HEADS_PER_KV * tq), BF16),
                        pltpu.VMEM((tk, HEADS_PER_KV * tq), F32),
                        pltpu.VMEM((tk, HEADS_PER_KV * tq), F32),
                        pltpu.VMEM((1, HEADS_PER_KV * tq), F32),
                        pltpu.VMEM((1, HEADS_PER_KV * tq), F32),
                        pltpu.VMEM((HEAD_DIM, HEADS_PER_KV * tq), F32)],
        compiler_params=pltpu.CompilerParams(dimension_semantics=("parallel", "parallel", "arbitrary"),
                                             vmem_limit_bytes=VMEM_LIMIT),
        name="attention",
    )(*args)


def _ctx_attn_kernel(*refs):
    q_ref, kv_ref, gate0_ref, gate1_ref, qw_ref, kw_ref = refs[:6]
    o_ref, kn_ref, vf_ref = refs[-3:]
    seq = q_ref.shape[0]
    q_scale = HEAD_DIM ** -0.5 * LOG2_E
    gates = (gate0_ref, gate1_ref)
    keys, values_t = [], []
    for g in range(ATT_KV_HEADS):
        dims = slice(g * HEAD_DIM, (g + 1) * HEAD_DIM)
        kn = _rms(kv_ref[:, dims].astype(F32), kw_ref[...])
        kn_ref[pl.ds(g, seq, stride=ATT_KV_HEADS), :] = kn
        keys.append(kn.astype(BF16))
        v = kv_ref[:, KV_WIDTH + g * HEAD_DIM:KV_WIDTH + (g + 1) * HEAD_DIM].astype(F32)
        vf_ref[pl.ds(g, seq, stride=ATT_KV_HEADS), :] = v
        values_t.append(v.T.astype(BF16))

    pending = {}
    for idx in range(ATT_HEADS + ATT_LOOKAHEAD):
        if idx < ATT_HEADS:
            dims = slice(idx * HEAD_DIM, (idx + 1) * HEAD_DIM)
            y = _rms(q_ref[:, dims].astype(F32), qw_ref[...]) * q_scale
            pending[idx] = jnp.dot(keys[idx // HEADS_PER_KV], y.T.astype(BF16), preferred_element_type=F32)
        head = idx - ATT_LOOKAHEAD
        if head >= 0:
            g, h = divmod(head, HEADS_PER_KV)
            s = pending.pop(head)
            p = jnp.exp2(s - jnp.max(s, axis=0, keepdims=True))
            l = jnp.sum(p, axis=0, keepdims=True)
            acc = jnp.dot(values_t[g], p.astype(BF16), preferred_element_type=F32)
            gate = gates[g][:, h * HEAD_DIM:(h + 1) * HEAD_DIM].astype(F32)
            o_ref[:, head * HEAD_DIM:(head + 1) * HEAD_DIM] = ((acc / l).T * _silu(gate)).astype(BF16)


def _context_attention(proj, q_norm_w, k_norm_w, batch, seq_len, collected, layer, depth):
    t = proj.shape[0]
    gw = HEADS_PER_KV * HEAD_DIM
    row_block = lambda width, col: pl.BlockSpec((seq_len, width), lambda b: (b, col // width))
    vec = pl.BlockSpec((1, HEAD_DIM), lambda b: (0, 0))
    in_specs = [row_block(ATT_WIDTH, COL_Q), row_block(2 * KV_WIDTH, COL_K),
                row_block(gw, COL_AG), row_block(gw, COL_AG + gw), vec, vec]
    args = [proj, proj, proj, proj, q_norm_w.reshape(1, HEAD_DIM), k_norm_w.reshape(1, HEAD_DIM)]
    aliases = {}
    if collected is not None:
        aliases = {len(args): 1, len(args) + 1: 2}
        in_specs += [pl.BlockSpec(memory_space=pl.ANY)] * 2
        args += list(collected)
    rows = seq_len * ATT_KV_HEADS
    slot = pl.BlockSpec((rows, HEAD_DIM), lambda b: (b * depth + layer, 0))
    att, keys, values = pl.pallas_call(
        _ctx_attn_kernel,
        grid=(batch,),
        in_specs=in_specs,
        out_specs=[row_block(ATT_WIDTH, 0), slot, slot],
        out_shape=[jax.ShapeDtypeStruct((t, ATT_WIDTH), BF16),
                   jax.ShapeDtypeStruct((batch * depth * rows, HEAD_DIM), F32),
                   jax.ShapeDtypeStruct((batch * depth * rows, HEAD_DIM), F32)],
        input_output_aliases=aliases,
        compiler_params=pltpu.CompilerParams(dimension_semantics=("parallel",),
                                             vmem_limit_bytes=VMEM_LIMIT),
        name="context_attention",
    )(*args)
    return att, (keys, values)


def _log_sigmoid(x):
    return -(jnp.maximum(-x, 0.0) + jnp.log(1.0 + jnp.exp(-jnp.abs(x))))


def _gla_level_table():
    t = np.arange(GLA_CHUNK)[:, None]
    s = np.arange(GLA_CHUNK)[None, :]
    lvl = np.full((GLA_CHUNK, GLA_CHUNK), len(GLA_LEVELS), np.int32)
    for i, size in reversed(list(enumerate(GLA_LEVELS))):
        lvl = np.where(t // size == s // size, i, lvl)
    fwd = np.where(s <= t, lvl, len(GLA_LEVELS)).astype(np.int32)
    return jnp.asarray(np.stack([fwd, fwd.T]))


def _minus_rows(b, picks, size):
    return jnp.concatenate(
        [b[i * size:(i + 1) * size] if p is None else b[i * size:(i + 1) * size] - b[p:p + 1]
         for i, p in enumerate(picks)], axis=0)


def _gla_decays(lr, a, bias):
    logits = jnp.dot(lr, a, preferred_element_type=F32) + bias
    return _log_sigmoid(logits) * (1.0 / GLA_TAU)


def _gla_cumsum(g, lvl):
    g_hi = g.astype(BF16)
    g_lo = (g - g_hi.astype(F32)).astype(BF16)
    tri = jnp.where(lvl < len(GLA_LEVELS), 1.0, 0.0).astype(BF16)
    return jnp.dot(tri, g_hi, preferred_element_type=F32) + jnp.dot(tri, g_lo, preferred_element_type=F32)


def _gla_operands(qk, b, rev):
    c = GLA_CHUNK
    q = qk[:, :GLA_KW].astype(F32) * GLA_DK ** -0.5
    k = qk[:, GLA_KW:].astype(F32)
    b_total = b[0:1] if rev else b[c - 1:c]
    q_inter = (q * jnp.exp(b)).astype(BF16)
    k_state = (k * jnp.exp(b_total - b)).astype(BF16)
    decay_total = jnp.exp(b_total)

    row = lax.broadcasted_iota(jnp.int32, (c, GLA_KW), 0)
    pairs = []
    for size in GLA_LEVELS:
        n = c // size
        if size == GLA_LEVELS[0]:
            if rev:
                picks = [(i + 1) * size if i + 1 < n else None for i in range(n)]
            else:
                picks = [i * size - 1 if i > 0 else None for i in range(n)]
            d = _minus_rows(b, picks, size)
            q_hat = q * jnp.exp(d)
            k_hat = k * jnp.exp(jnp.minimum(-d, GLA_EXP_CLAMP))
        else:
            half = size // 2
            d = _minus_rows(b, [i * size + half - (0 if rev else 1) for i in range(n)], size)
            late = (row & half) != 0
            q_side = jnp.logical_not(late) if rev else late
            q_hat = q * jnp.exp(jnp.where(q_side, d, -jnp.inf))
            k_hat = k * jnp.exp(jnp.where(q_side, -jnp.inf, -d))
        pairs.append((q_hat.astype(BF16), k_hat.astype(BF16)))
    return q_inter, k_state, decay_total, pairs


def _gla_kernel(*refs):
    qkf_ref, vf_ref, lrf_ref, qkb_ref, vb_ref, lrb_ref, a_ref, bias_ref, lvl_ref, s0_ref = refs[:10]
    of_ref, ob_ref, so_ref, st_ref = refs[-4:]
    j = pl.program_id(1)

    @pl.when(j == 0)
    def _():
        for z in range(2):
            for h in range(GLA_HEADS):
                st_ref[z, h] = s0_ref[0, z, h].T

    n_chunks = qkf_ref.shape[0] // GLA_CHUNK
    chunk_rows = [slice(c * GLA_CHUNK, (c + 1) * GLA_CHUNK) for c in range(n_chunks)]
    units = [(z, rows) for rows_f, rows_b in zip(chunk_rows, reversed(chunk_rows))
             for z, rows in ((0, rows_f), (1, rows_b))]
    src = ((qkf_ref, vf_ref, lrf_ref, of_ref), (qkb_ref, vb_ref, lrb_ref, ob_ref))
    n_levels = len(GLA_LEVELS)
    hk = [slice(h * GLA_DK, (h + 1) * GLA_DK) for h in range(GLA_HEADS)]
    hv = [slice(h * GLA_DV, (h + 1) * GLA_DV) for h in range(GLA_HEADS)]
    tn = (((0,), (0,)), ((), ()))

    g = [_gla_decays(src[z][2][rows, :], a_ref[z], bias_ref[z]) for z, rows in units]
    b = [_gla_cumsum(gi, lvl_ref[z]) for gi, (z, rows) in zip(g, units)]
    v = [src[z][1][rows, :] for z, rows in units]
    state = [[st_ref[z, h] for h in range(GLA_HEADS)] for z in range(2)]
    ops, delta, scores = {}, {}, {}

    def queue_scores(u):
        z, rows = units[u]
        ops[u] = _gla_operands(src[z][0][rows, :], b[u], z == 1)
        delta[u] = [lax.dot_general(v[u][:, hv[h]], ops[u][1][:, hk[h]], tn, preferred_element_type=F32)
                    for h in range(GLA_HEADS)]
        scores[u] = [[lax.dot_general(qh[:, hk[h]], kh[:, hk[h]], _NT, preferred_element_type=F32)
                      for qh, kh in ops[u][3]] for h in range(GLA_HEADS)]

    def consume(u):
        z, rows = units[u]
        lvl = lvl_ref[z]
        q_inter, _, decay_total, _ = ops.pop(u)
        for h in range(GLA_HEADS):
            p = scores[u][h]
            att = p[-1]
            for i in reversed(range(n_levels - 1)):
                att = jnp.where(lvl == i, p[i], att)
            o = jnp.dot(att.astype(BF16), v[u][:, hv[h]], preferred_element_type=F32) + lax.dot_general(
                q_inter[:, hk[h]], state[z][h].astype(BF16), _NT, preferred_element_type=F32)
            src[z][3][rows, hv[h]] = o.astype(BF16)
            state[z][h] = state[z][h] * decay_total[:, hk[h]] + delta[u][h]

    for u in range(len(units)):
        queue_scores(u)
        if u > 0:
            consume(u - 1)
    consume(len(units) - 1)
    for z in range(2):
        for h in range(GLA_HEADS):
            st_ref[z, h] = state[z][h]

    @pl.when(j == pl.num_programs(1) - 1)
    def _():
        for z in range(2):
            for h in range(GLA_HEADS):
                so_ref[0, 0, z, h] = st_ref[z, h].T


def _gla(proj, lr, a_pad, bias, s0, batch, seq_len, states=None, layer=0, depth=1):
    t = proj.shape[0]
    blk = min(GLA_BLOCK, seq_len)
    nb = seq_len // blk
    fwd = lambda b, j: b * nb + j
    bwd = lambda b, j: b * nb + nb - 1 - j
    qk_w = 2 * GLA_KW

    def proj_specs(row):
        return [pl.BlockSpec((blk, qk_w), lambda b, j: (row(b, j), COL_GQK // qk_w)),
                pl.BlockSpec((blk, GLA_VW), lambda b, j: (row(b, j), COL_GV // GLA_VW)),
                pl.BlockSpec((blk, LR_BLOCK), lambda b, j: (row(b, j), 0))]

    state = (2, GLA_HEADS, GLA_DK, GLA_DV)
    in_specs = proj_specs(fwd) + proj_specs(bwd) + [
        pl.BlockSpec((2, LR_BLOCK, GLA_KW), lambda b, j: (0, 0, 0)),
        pl.BlockSpec((2, 1, GLA_KW), lambda b, j: (0, 0, 0)),
        pl.BlockSpec((2, GLA_CHUNK, GLA_CHUNK), lambda b, j: (0, 0, 0)),
        pl.BlockSpec((1,) + state, lambda b, j: (b, 0, 0, 0, 0))]
    args = [proj, proj, lr, proj, proj, lr, a_pad, bias, _gla_level_table(), s0]
    aliases = {}
    if states is not None:
        aliases = {len(args): 2}
        in_specs.append(pl.BlockSpec(memory_space=pl.ANY))
        args.append(states)
    return pl.pallas_call(
        _gla_kernel,
        grid=(batch, nb),
        in_specs=in_specs,
        out_specs=[pl.BlockSpec((blk, GLA_VW), lambda b, j: (fwd(b, j), 0)),
                   pl.BlockSpec((blk, GLA_VW), lambda b, j: (bwd(b, j), 0)),
                   pl.BlockSpec((1, 1) + state, lambda b, j: (b, layer, 0, 0, 0, 0))],
        out_shape=[jax.ShapeDtypeStruct((t, GLA_VW), BF16),
                   jax.ShapeDtypeStruct((t, GLA_VW), BF16),
                   jax.ShapeDtypeStruct((batch, depth) + state, F32)],
        scratch_shapes=[pltpu.VMEM((2, GLA_HEADS, GLA_DV, GLA_DK), F32)],
        input_output_aliases=aliases,
        compiler_params=pltpu.CompilerParams(dimension_semantics=("parallel", "arbitrary"),
                                             vmem_limit_bytes=VMEM_LIMIT),
        name="gla",
    )(*args)


HALO = 16


def _out_kernel(*refs, tm, seq_len, final):
    (x_ref, a_ref, ch_ref, cb_ref, cc_ref, cg_ref, chp_ref, ccp_ref, chn_ref, ccn_ref,
     gg_ref, of_ref, ob_ref, cw_ref, gnw_ref, gate_ref, w_ref) = refs[:17]
    pos = 17
    if final:
        fnw_ref = refs[pos]
        pos += 1
    o_ref = refs[pos]

    def prod(c_ref, h_ref, r):
        return c_ref[r:r + 1, :].astype(F32) * h_ref[r:r + 1, :].astype(F32)

    u = cc_ref[...].astype(F32) * ch_ref[...].astype(F32)
    row = lax.broadcasted_iota(jnp.int32, u.shape, 0)
    seq_pos = (pl.program_id(0) * tm + row) & (seq_len - 1)
    u_prev = jnp.where(row == 0, prod(ccp_ref, chp_ref, HALO - 1), pltpu.roll(u, 1, axis=0))
    u_prev = jnp.where(seq_pos == 0, 0.0, u_prev)
    u_next = jnp.where(row == tm - 1, prod(ccn_ref, chn_ref, 0), pltpu.roll(u, tm - 1, axis=0))
    u_next = jnp.where(seq_pos == seq_len - 1, 0.0, u_next)
    conv = u_prev * cw_ref[0:1, :] + u * cw_ref[1:2, :] + u_next * cw_ref[2:3, :]
    out_b = cb_ref[...].astype(F32) * conv * _silu(cg_ref[...].astype(F32))
    parts = [a_ref[...], out_b.astype(BF16)]

    o = of_ref[...].astype(F32) + ob_ref[...].astype(F32)
    gg = gg_ref[...].astype(F32)
    for h in range(GLA_HEADS):
        cols = slice(h * GLA_DV, (h + 1) * GLA_DV)
        parts.append((_rms(o[:, cols], gnw_ref[...]) * _silu(gg[:, cols])).astype(BF16))

    r = jnp.dot(jnp.concatenate(parts, axis=1), w_ref[0], preferred_element_type=F32)
    xn = x_ref[...] + gate_ref[0] * r
    if final:
        xn = _rms(xn, fnw_ref[...])
    o_ref[...] = xn


def _out_projection(x, att, proj, o_f, o_b, conv_w, gla_norm_w, mod3, w_out, layer, final_norm_w, mod_row,
                    seq_len):
    t, d = x.shape
    tm = 512
    cw = conv_w.shape[1]
    final = final_norm_w is not None
    per_halo = tm // HALO
    last_halo = t // HALO - 1

    def col(c, width):
        return lambda i: (i, c // width)

    prev = lambda c: (lambda i: (jnp.maximum(i * per_halo - 1, 0), c // cw))
    nxt = lambda c: (lambda i: (jnp.minimum((i + 1) * per_halo, last_halo), c // cw))
    in_specs = [pl.BlockSpec((tm, d), lambda i: (i, 0)),
                pl.BlockSpec((tm, ATT_WIDTH), lambda i: (i, 0)),
                pl.BlockSpec((tm, cw), col(COL_CH, cw)),
                pl.BlockSpec((tm, cw), col(COL_CB, cw)),
                pl.BlockSpec((tm, cw), col(COL_CC, cw)),
                pl.BlockSpec((tm, cw), col(COL_CG, cw)),
                pl.BlockSpec((HALO, cw), prev(COL_CH)),
                pl.BlockSpec((HALO, cw), prev(COL_CC)),
                pl.BlockSpec((HALO, cw), nxt(COL_CH)),
                pl.BlockSpec((HALO, cw), nxt(COL_CC)),
                pl.BlockSpec((tm, GLA_VW), col(COL_GG, GLA_VW)),
                pl.BlockSpec((tm, GLA_VW), lambda i: (i, 0)),
                pl.BlockSpec((tm, GLA_VW), lambda i: (i, 0)),
                pl.BlockSpec((3, cw), lambda i: (0, 0)),
                pl.BlockSpec((1, GLA_DV), lambda i: (0, 0)),
                pl.BlockSpec((1, 1, d), lambda i: (mod_row(i * tm), 0, 2)),
                pl.BlockSpec((1,) + w_out.shape[1:], lambda i: (layer, 0, 0))]
    args = [x, att, proj, proj, proj, proj, proj, proj, proj, proj, proj, o_f, o_b,
            conv_w, gla_norm_w.reshape(1, GLA_DV), mod3, w_out]
    if final:
        in_specs.append(pl.BlockSpec((1, d), lambda i: (0, 0)))
        args.append(final_norm_w.reshape(1, d))
    return pl.pallas_call(
        functools.partial(_out_kernel, tm=tm, seq_len=seq_len, final=final),
        grid=(t // tm,),
        in_specs=in_specs,
        out_specs=pl.BlockSpec((tm, d), lambda i: (i, 0)),
        out_shape=jax.ShapeDtypeStruct((t, d), F32),
        compiler_params=pltpu.CompilerParams(dimension_semantics=("parallel",),
                                             vmem_limit_bytes=VMEM_LIMIT),
        name="out_projection",
    )(*args)


def _rope_tables(seq_len):
    pos = jnp.arange(seq_len)
    row = (pos // GRID_W).astype(F32)
    col = (pos % GRID_W).astype(F32)
    n_freq = HEAD_DIM // 4
    inv = ROPE_THETA ** (-jnp.arange(n_freq, dtype=F32) / n_freq)
    ang_r = row[:, None] * inv[None, :]
    ang_c = col[:, None] * inv[None, :]
    cos = jnp.concatenate([jnp.cos(ang_r), jnp.cos(ang_r), jnp.cos(ang_c), jnp.cos(ang_c)], axis=-1)
    sin = jnp.concatenate([-jnp.sin(ang_r), jnp.sin(ang_r), -jnp.sin(ang_c), jnp.sin(ang_c)], axis=-1)
    return cos, sin


@jax.jit
def _forward(x_prompt, x_sample, cache_k, cache_v, state_gla, c, c_ctx, w_mod, b_mod, norm_w, w_in,
             q_norm_w, k_norm_w, conv_w, gla_a_up, gla_a_bias, gla_norm_w, w_out, final_norm_w):
    batch, seq, d = x_prompt.shape
    dec_batch, dec_seq, _ = x_sample.shape
    depth = w_in.shape[0]
    past = cache_k.shape[2]

    cond = jnp.zeros((8, d), F32).at[0].set(c_ctx).at[1:1 + dec_batch].set(c)
    mod3 = _modulation(cond, w_mod, b_mod).reshape(depth * 8, 1, 3 * d)
    rope_tables = _rope_tables(dec_seq)

    h = x_prompt.reshape(batch * seq, d)
    z = x_sample.reshape(dec_batch * dec_seq, d)
    zero_state = jnp.zeros((batch, 2, GLA_HEADS, GLA_DK, GLA_DV), F32)
    w_in_t = jnp.swapaxes(w_in, 1, 2).astype(BF16)
    w_lr_t = jnp.pad(w_in_t[:, COL_LR:, :], ((0, 0), (0, LR_BLOCK - (IN_WIDTH - COL_LR)), (0, 0)))
    w_out_b = w_out.astype(BF16)
    new_kv, new_state = None, None
    for l in range(depth):
        a_pad = jnp.zeros((2, LR_BLOCK, GLA_KW), F32)
        for zdir in range(2):
            a_pad = a_pad.at[zdir, zdir * GLA_RANK:(zdir + 1) * GLA_RANK].set(gla_a_up[l, zdir])
        a_pad = a_pad.astype(BF16)
        bias = gla_a_bias[l].reshape(2, 1, GLA_KW)
        final_w = final_norm_w if l == depth - 1 else None

        row_ctx = lambda tok, l=l: l * 8
        proj, lr = _in_projection(h, mod3, norm_w[l], w_in_t, w_lr_t, l, row_ctx)
        att, new_kv = _context_attention(proj, q_norm_w[l], k_norm_w[l], batch, seq, new_kv, l, depth)
        o_f, o_b, new_state = _gla(proj, lr, a_pad, bias, zero_state, batch, seq, new_state, l, depth)
        h = _out_projection(h, att, proj, o_f, o_b, conv_w[l], gla_norm_w[l], mod3, w_out_b, l, final_w,
                            row_ctx, seq)

        row_lat = lambda tok, l=l: l * 8 + 1 + tok // dec_seq
        proj, lr = _in_projection(z, mod3, norm_w[l], w_in_t, w_lr_t, l, row_lat)
        ctx_kv = (cache_k[:, l].reshape(dec_batch * past, KV_WIDTH),
                  cache_v[:, l].reshape(dec_batch * past, KV_WIDTH))
        k_all, vt_all, k_norm2 = _kv_prepare(proj, k_norm_w[l], rope_tables, ctx_kv, dec_batch, dec_seq)
        att = _attention(proj, k_all, vt_all, k_norm2, q_norm_w[l], rope_tables, dec_batch, dec_seq)
        o_f, o_b, _ = _gla(proj, lr, a_pad, bias, state_gla[:, l], dec_batch, dec_seq)
        z = _out_projection(z, att, proj, o_f, o_b, conv_w[l], gla_norm_w[l], mod3, w_out_b, l, final_w,
                            row_lat, dec_seq)

    kv_shape = (batch, depth, seq, ATT_KV_HEADS, HEAD_DIM)
    return (h.reshape(batch, seq, d), z.reshape(dec_batch, dec_seq, d),
            new_kv[0].reshape(kv_shape), new_kv[1].reshape(kv_shape), new_state)


def kernel(x_prompt, x_sample, cache_k, cache_v, state_gla, c, c_ctx, w_mod, b_mod, norm_w, w_in, q_norm_w,
           k_norm_w, conv_w, gla_a_up, gla_a_bias, gla_norm_w, w_out, final_norm_w):
    return _forward(x_prompt, x_sample, cache_k, cache_v, state_gla, c, c_ctx, w_mod, b_mod, norm_w, w_in,
                    q_norm_w, k_norm_w, conv_w, gla_a_up, gla_a_bias, gla_norm_w, w_out, final_norm_w)
```

```python
import functools

import numpy as np
import jax
import jax.numpy as jnp
from jax import lax
from jax.experimental import pallas as pl
from jax.experimental.pallas import tpu as pltpu

F32 = jnp.float32
BF16 = jnp.bfloat16

GRID_W = 64
HEAD_DIM = 128
ATT_HEADS = 8
ATT_KV_HEADS = 2
HEADS_PER_KV = ATT_HEADS // ATT_KV_HEADS
ATT_WIDTH = ATT_HEADS * HEAD_DIM
KV_WIDTH = ATT_KV_HEADS * HEAD_DIM
GLA_HEADS = 4
GLA_DK = 64
GLA_DV = 128
GLA_KW = GLA_HEADS * GLA_DK
GLA_VW = GLA_HEADS * GLA_DV
GLA_RANK = 16
GLA_TAU = 16.0
ROPE_THETA = 10000.0
LOG2_E = 1.4426950408889634
ATT_DIRECT_EXP_LIMIT = 48.0
ATT_LOOKAHEAD = 2
CTX_LOOKAHEAD = 4
EPS = 1e-6

COL_Q = 0
COL_K = 1024
COL_V = 1280
COL_AG = 1536
COL_CH = 2560
COL_CB = 3072
COL_CC = 3584
COL_CG = 4096
COL_GQK = 4608
COL_GV = 5120
COL_GG = 5632
COL_LR = 6144
IN_WIDTH = 6176
IN_TILE = 2048
IN_ROW_CHUNKS = 4
IN_MAIN = COL_LR
LR_BLOCK = 128

GLA_CHUNK = 128
GLA_LEVELS = (16, 32, 64, 128)
GLA_BLOCK = 256
GLA_EXP_CLAMP = 60.0

VMEM_LIMIT = 56 * 1024 * 1024


def _silu(x):
    return x * jax.nn.sigmoid(x)


def _rms(x, w):
    return x * lax.rsqrt(jnp.mean(x * x, axis=-1, keepdims=True) + EPS) * w


def _rope(y, cos, sin_signed):
    lane = lax.broadcasted_iota(jnp.int32, y.shape, 1)
    partner = jnp.where((lane & 63) < 32, pltpu.roll(y, 96, axis=1), pltpu.roll(y, 32, axis=1))
    return y * cos + partner * sin_signed


def _mod_kernel(c_ref, w_ref, b_ref, o_ref):
    a = _silu(c_ref[...]).astype(BF16)
    o_ref[0] = jnp.dot(a, w_ref[0].astype(BF16), preferred_element_type=F32) + b_ref[0]


def _modulation(cond, w_mod, b_mod):
    depth, d, n = w_mod.shape
    tn = 768
    return pl.pallas_call(
        _mod_kernel,
        grid=(depth, n // tn),
        in_specs=[pl.BlockSpec((8, d), lambda l, j: (0, 0)),
                  pl.BlockSpec((1, d, tn), lambda l, j: (l, 0, j)),
                  pl.BlockSpec((1, 1, tn), lambda l, j: (l, 0, j))],
        out_specs=pl.BlockSpec((1, 8, tn), lambda l, j: (l, 0, j)),
        out_shape=jax.ShapeDtypeStruct((depth, 8, n), F32),
        compiler_params=pltpu.CompilerParams(dimension_semantics=("parallel", "parallel"),
                                             vmem_limit_bytes=VMEM_LIMIT),
        name="modulation",
    )(cond, w_mod, b_mod.reshape(depth, 1, n))


_NT = (((1,), (1,)), ((), ()))


def _in_kernel(x_ref, nw_ref, shift_ref, scale_ref, w_ref, wlr_ref, o_ref, lr_ref, xn_ref):
    def project(xn, w):
        return lax.dot_general(xn, w, _NT, preferred_element_type=F32).astype(BF16)

    @pl.when(pl.program_id(1) == 0)
    def _():
        gain = nw_ref[...] * (1.0 + scale_ref[0])
        per = x_ref.shape[0] // IN_ROW_CHUNKS
        for r in range(IN_ROW_CHUNKS):
            rows = slice(r * per, (r + 1) * per)
            x = x_ref[rows, :]
            inv = lax.rsqrt(jnp.mean(x * x, axis=-1, keepdims=True) + EPS)
            xn = (x * inv * gain + shift_ref[0]).astype(BF16)
            xn_ref[rows, :] = xn
            o_ref[rows, :] = project(xn, w_ref[0])
            lr_ref[rows, :] = project(xn, wlr_ref[0])

    @pl.when(pl.program_id(1) > 0)
    def _():
        o_ref[...] = project(xn_ref[...], w_ref[0])


def _in_projection(x, mod3, norm_w, w_in_t, w_lr_t, layer, mod_row):
    t, d = x.shape
    tm = 1024
    return pl.pallas_call(
        _in_kernel,
        grid=(t // tm, IN_MAIN // IN_TILE),
        in_specs=[pl.BlockSpec((tm, d), lambda i, j: (i, 0)),
                  pl.BlockSpec((1, d), lambda i, j: (0, 0)),
                  pl.BlockSpec((1, 1, d), lambda i, j: (mod_row(i * tm), 0, 0)),
                  pl.BlockSpec((1, 1, d), lambda i, j: (mod_row(i * tm), 0, 1)),
                  pl.BlockSpec((1, IN_TILE, d), lambda i, j: (layer, j, 0)),
                  pl.BlockSpec((1, LR_BLOCK, d), lambda i, j: (layer, 0, 0))],
        out_specs=[pl.BlockSpec((tm, IN_TILE), lambda i, j: (i, j)),
                   pl.BlockSpec((tm, LR_BLOCK), lambda i, j: (i, 0))],
        out_shape=[jax.ShapeDtypeStruct((t, IN_MAIN), BF16),
                   jax.ShapeDtypeStruct((t, LR_BLOCK), BF16)],
        scratch_shapes=[pltpu.VMEM((tm, d), BF16)],
        compiler_params=pltpu.CompilerParams(dimension_semantics=("parallel", "arbitrary"),
                                             vmem_limit_bytes=VMEM_LIMIT),
        name="in_projection",
    )(x, norm_w.reshape(1, d), mod3, mod3, w_in_t, w_lr_t)


def _kv_kernel(kv_ref, kw_ref, cos_ref, sin_ref, ck_ref, cv_ref, k_ref, vt_ref, kmax_ref, *, n_self):
    @pl.when(pl.program_id(1) == 0)
    def _():
        kmax_ref[...] = jnp.zeros(kmax_ref.shape, F32)

    def put_keys(g, kb):
        cols = slice(g * HEAD_DIM, (g + 1) * HEAD_DIM)
        k_ref[:, cols] = kb
        kf = kb.astype(F32)
        n2 = jnp.max(jnp.sum(kf * kf, axis=1, keepdims=True), axis=0, keepdims=True)
        kmax_ref[0, g:g + 1, :] = jnp.maximum(kmax_ref[0, g:g + 1, :], jnp.broadcast_to(n2, (1, HEAD_DIM)))

    @pl.when(pl.program_id(1) < n_self)
    def _():
        for g in range(ATT_KV_HEADS):
            cols = slice(g * HEAD_DIM, (g + 1) * HEAD_DIM)
            y = _rope(_rms(kv_ref[:, cols].astype(F32), kw_ref[...]), cos_ref[...], sin_ref[...])
            put_keys(g, y.astype(BF16))
        vt_ref[...] = kv_ref[:, KV_WIDTH:].astype(F32).T.astype(BF16)

    @pl.when(pl.program_id(1) >= n_self)
    def _():
        for g in range(ATT_KV_HEADS):
            put_keys(g, ck_ref[:, g * HEAD_DIM:(g + 1) * HEAD_DIM].astype(BF16))
        vt_ref[...] = cv_ref[...].T.astype(BF16)


def _kv_prepare(proj, k_norm_w, rope_tables, ctx_kv, batch, seq_len):
    tr = 512
    n_self = seq_len // tr
    n_ctx = ctx_kv[0].shape[0] // batch // tr
    n_all = n_self + n_ctx
    own = lambda j: jnp.minimum(j, n_self - 1)
    ctx_spec = pl.BlockSpec((tr, KV_WIDTH), lambda b, j: (b * n_ctx + jnp.maximum(j - n_self, 0), 0))
    rope_spec = pl.BlockSpec((tr, HEAD_DIM), lambda b, j: (own(j), 0))
    return pl.pallas_call(
        functools.partial(_kv_kernel, n_self=n_self),
        grid=(batch, n_all),
        in_specs=[pl.BlockSpec((tr, 2 * KV_WIDTH), lambda b, j: (b * n_self + own(j), COL_K // (2 * KV_WIDTH))),
                  pl.BlockSpec((1, HEAD_DIM), lambda b, j: (0, 0)),
                  rope_spec, rope_spec, ctx_spec, ctx_spec],
        out_specs=[pl.BlockSpec((tr, KV_WIDTH), lambda b, j: (b * n_all + j, 0)),
                   pl.BlockSpec((KV_WIDTH, tr), lambda b, j: (b, j)),
                   pl.BlockSpec((1, 8, HEAD_DIM), lambda b, j: (b, 0, 0))],
        out_shape=[jax.ShapeDtypeStruct((batch * n_all * tr, KV_WIDTH), BF16),
                   jax.ShapeDtypeStruct((batch * KV_WIDTH, n_all * tr), BF16),
                   jax.ShapeDtypeStruct((batch, 8, HEAD_DIM), F32)],
        compiler_params=pltpu.CompilerParams(dimension_semantics=("parallel", "arbitrary")),
        name="kv_prepare",
    )(proj, k_norm_w.reshape(1, HEAD_DIM), *rope_tables, *ctx_kv)


def _attn_kernel(q_ref, gate_ref, k_ref, vt_ref, kmax_ref, qw_ref, cos_ref, sin_ref,
                 o_ref, qt_ref, s0_ref, s1_ref, m_ref, l_ref, acc_ref, *, tq, tk, n_chunks):
    q_scale = HEAD_DIM ** -0.5 * LOG2_E
    heads = [slice(h * tq, (h + 1) * tq) for h in range(HEADS_PER_KV)]
    dims = [slice(h * HEAD_DIM, (h + 1) * HEAD_DIM) for h in range(HEADS_PER_KV)]

    def rows(c):
        off = c * tk
        return pl.ds(off if isinstance(off, int) else pl.multiple_of(off, tk), tk)

    def prepare(h):
        y = _rope(_rms(q_ref[:, dims[h]].astype(F32), qw_ref[...]), cos_ref[...], sin_ref[...])
        qt_ref[:, heads[h]] = (y * q_scale).T.astype(BF16)
        l_ref[:, heads[h]] = jnp.zeros((1, tq), F32)
        acc_ref[:, heads[h]] = jnp.zeros((HEAD_DIM, tq), F32)

    def finish(h):
        o = (acc_ref[:, heads[h]] / l_ref[:, heads[h]]).T
        o_ref[:, dims[h]] = (o * _silu(gate_ref[:, dims[h]].astype(F32))).astype(BF16)

    w = qw_ref[...]
    q_norm2 = HEAD_DIM * q_scale * q_scale * jnp.max(w * w)
    key_norm2 = jnp.max(kmax_ref[0, pl.ds(pl.program_id(1), 1), :])
    bounded = q_norm2 * key_norm2 < ATT_DIRECT_EXP_LIMIT ** 2

    @pl.when(bounded)
    def _():
        pieces = [(h, c) for h in range(HEADS_PER_KV) for c in range(n_chunks)]
        pending = {}
        for idx in range(len(pieces) + ATT_LOOKAHEAD):
            if idx < len(pieces):
                h, c = pieces[idx]
                if c == 0:
                    prepare(h)
                pending[idx] = jnp.dot(k_ref[rows(c), :], qt_ref[:, heads[h]], preferred_element_type=F32)
            if idx >= ATT_LOOKAHEAD:
                h, c = pieces[idx - ATT_LOOKAHEAD]
                p = jnp.exp2(pending.pop(idx - ATT_LOOKAHEAD))
                l_ref[:, heads[h]] += jnp.sum(p, axis=0, keepdims=True)
                acc_ref[:, heads[h]] += jnp.dot(vt_ref[:, rows(c)], p.astype(BF16), preferred_element_type=F32)
                if c == n_chunks - 1:
                    finish(h)

    @pl.when(jnp.logical_not(bounded))
    def _():
        for h in range(HEADS_PER_KV):
            prepare(h)
        m_ref[...] = jnp.full(m_ref.shape, -jnp.inf, F32)

        def scores(c, s_ref):
            s_ref[...] = jnp.dot(k_ref[rows(c), :], qt_ref[...], preferred_element_type=F32)

        def update(c, s_ref):
            s = s_ref[...]
            m_old = m_ref[...]
            m_new = jnp.maximum(m_old, jnp.max(s, axis=0, keepdims=True))
            p = jnp.exp2(s - m_new)
            alpha = jnp.exp2(m_old - m_new)
            l_ref[...] = alpha * l_ref[...] + jnp.sum(p, axis=0, keepdims=True)
            pv = jnp.dot(vt_ref[:, rows(c)], p.astype(BF16), preferred_element_type=F32)
            acc_ref[...] = alpha * acc_ref[...] + pv
            m_ref[...] = m_new

        scores(0, s0_ref)
        n_pairs = (n_chunks - 1) // 2

        def body(i, carry):
            c = 2 * i
            scores(c + 1, s1_ref)
            update(c, s0_ref)
            scores(c + 2, s0_ref)
            update(c + 1, s1_ref)
            return carry

        lax.fori_loop(0, n_pairs, body, 0)
        done = 2 * n_pairs
        if n_chunks - done == 2:
            scores(done + 1, s1_ref)
            update(done, s0_ref)
            update(done + 1, s1_ref)
        else:
            update(done, s0_ref)
        for h in range(HEADS_PER_KV):
            finish(h)


def _attention(proj, k_all, vt_all, k_norm2, q_norm_w, rope_tables, batch, seq_len):
    t = proj.shape[0]
    keys = k_all.shape[0] // batch
    tq = 512
    tk = min(512, keys)
    nq = seq_len // tq
    gw = HEADS_PER_KV * HEAD_DIM
    rope_spec = pl.BlockSpec((tq, HEAD_DIM), lambda b, g, i: (i, 0))
    in_specs = [pl.BlockSpec((tq, gw), lambda b, g, i: (b * nq + i, COL_Q // gw + g)),
                pl.BlockSpec((tq, gw), lambda b, g, i: (b * nq + i, COL_AG // gw + g)),
                pl.BlockSpec((keys, HEAD_DIM), lambda b, g, i: (b, g)),
                pl.BlockSpec((HEAD_DIM, keys), lambda b, g, i: (b * ATT_KV_HEADS + g, 0)),
                pl.BlockSpec((1,) + k_norm2.shape[1:], lambda b, g, i: (b, 0, 0)),
                pl.BlockSpec((1, HEAD_DIM), lambda b, g, i: (0, 0)),
                rope_spec, rope_spec]
    args = [proj, proj, k_all, vt_all, k_norm2, q_norm_w.reshape(1, HEAD_DIM), *rope_tables]
    return pl.pallas_call(
        functools.partial(_attn_kernel, tq=tq, tk=tk, n_chunks=keys // tk),
        grid=(batch, ATT_KV_HEADS, nq),
        in_specs=in_specs,
        out_specs=pl.BlockSpec((tq, gw), lambda b, g, i: (b * nq + i, g)),
        out_shape=jax.ShapeDtypeStruct((t, ATT_WIDTH), BF16),
        scratch_shapes=[pltpu.VMEM((HEAD_DIM, HEADS_PER_KV * tq), BF16),
                        pltpu.VMEM((tk, HEADS_PER_KV * tq), F32),
                        pltpu.VMEM((tk, HEADS_PER_KV * tq), F32),
                        pltpu.VMEM((1, HEADS_PER_KV * tq), F32),
                        pltpu.VMEM((1, HEADS_PER_KV * tq), F32),
                        pltpu.VMEM((HEAD_DIM, HEADS_PER_KV * tq), F32)],
        compiler_params=pltpu.CompilerParams(dimension_semantics=("parallel", "parallel", "arbitrary"),
                                             vmem_limit_bytes=VMEM_LIMIT),
        name="attention",
    )(*args)


def _ctx_attn_kernel(*refs):
    q_ref, kv_ref, gate0_ref, gate1_ref, qw_ref, kw_ref = refs[:6]
    o_ref, kn_ref, vf_ref = refs[-3:]
    seq = q_ref.shape[0]
    q_scale = HEAD_DIM ** -0.5 * LOG2_E
    gates = (gate0_ref, gate1_ref)
    keys, values_t = [], []
    for g in range(ATT_KV_HEADS):
        dims = slice(g * HEAD_DIM, (g + 1) * HEAD_DIM)
        kn = _rms(kv_ref[:, dims].astype(F32), kw_ref[...])
        kn_ref[pl.ds(g, seq, stride=ATT_KV_HEADS), :] = kn
        keys.append(kn.astype(BF16))
        v = kv_ref[:, KV_WIDTH + g * HEAD_DIM:KV_WIDTH + (g + 1) * HEAD_DIM].astype(F32)
        vf_ref[pl.ds(g, seq, stride=ATT_KV_HEADS), :] = v
        values_t.append(v.T.astype(BF16))

    pending = {}
    for idx in range(ATT_HEADS + CTX_LOOKAHEAD):
        if idx < ATT_HEADS:
            dims = slice(idx * HEAD_DIM, (idx + 1) * HEAD_DIM)
            y = _rms(q_ref[:, dims].astype(F32), qw_ref[...]) * q_scale
            pending[idx] = jnp.dot(keys[idx // HEADS_PER_KV], y.T.astype(BF16), preferred_element_type=F32)
        head = idx - CTX_LOOKAHEAD
        if head >= 0:
            g, h = divmod(head, HEADS_PER_KV)
            s = pending.pop(head)
            p = jnp.exp2(s - jnp.max(s, axis=0, keepdims=True))
            l = jnp.sum(p, axis=0, keepdims=True)
            acc = jnp.dot(values_t[g], p.astype(BF16), preferred_element_type=F32)
            gate = gates[g][:, h * HEAD_DIM:(h + 1) * HEAD_DIM].astype(F32)
            o_ref[:, head * HEAD_DIM:(head + 1) * HEAD_DIM] = ((acc / l).T * _silu(gate)).astype(BF16)


def _context_attention(proj, q_norm_w, k_norm_w, batch, seq_len, collected, layer, depth):
    t = proj.shape[0]
    gw = HEADS_PER_KV * HEAD_DIM
    row_block = lambda width, col: pl.BlockSpec((seq_len, width), lambda b: (b, col // width))
    vec = pl.BlockSpec((1, HEAD_DIM), lambda b: (0, 0))
    in_specs = [row_block(ATT_WIDTH, COL_Q), row_block(2 * KV_WIDTH, COL_K),
                row_block(gw, COL_AG), row_block(gw, COL_AG + gw), vec, vec]
    args = [proj, proj, proj, proj, q_norm_w.reshape(1, HEAD_DIM), k_norm_w.reshape(1, HEAD_DIM)]
    aliases = {}
    if collected is not None:
        aliases = {len(args): 1, len(args) + 1: 2}
        in_specs += [pl.BlockSpec(memory_space=pl.ANY)] * 2
        args += list(collected)
    rows = seq_len * ATT_KV_HEADS
    slot = pl.BlockSpec((rows, HEAD_DIM), lambda b: (b * depth + layer, 0))
    att, keys, values = pl.pallas_call(
        _ctx_attn_kernel,
        grid=(batch,),
        in_specs=in_specs,
        out_specs=[row_block(ATT_WIDTH, 0), slot, slot],
        out_shape=[jax.ShapeDtypeStruct((t, ATT_WIDTH), BF16),
                   jax.ShapeDtypeStruct((batch * depth * rows, HEAD_DIM), F32),
                   jax.ShapeDtypeStruct((batch * depth * rows, HEAD_DIM), F32)],
        input_output_aliases=aliases,
        compiler_params=pltpu.CompilerParams(dimension_semantics=("parallel",),
                                             vmem_limit_bytes=VMEM_LIMIT),
        name="context_attention",
    )(*args)
    return att, (keys, values)


def _log_sigmoid(x):
    return -(jnp.maximum(-x, 0.0) + jnp.log(1.0 + jnp.exp(-jnp.abs(x))))


def _gla_level_table():
    t = np.arange(GLA_CHUNK)[:, None]
    s = np.arange(GLA_CHUNK)[None, :]
    lvl = np.full((GLA_CHUNK, GLA_CHUNK), len(GLA_LEVELS), np.int32)
    for i, size in reversed(list(enumerate(GLA_LEVELS))):
        lvl = np.where(t // size == s // size, i, lvl)
    fwd = np.where(s <= t, lvl, len(GLA_LEVELS)).astype(np.int32)
    return jnp.asarray(np.stack([fwd, fwd.T]))


def _minus_rows(b, picks, size):
    return jnp.concatenate(
        [b[i * size:(i + 1) * size] if p is None else b[i * size:(i + 1) * size] - b[p:p + 1]
         for i, p in enumerate(picks)], axis=0)


def _gla_decays(lr, a, bias):
    logits = jnp.dot(lr, a, preferred_element_type=F32) + bias
    return _log_sigmoid(logits) * (1.0 / GLA_TAU)


def _gla_cumsum(g, lvl):
    g_hi = g.astype(BF16)
    g_lo = (g - g_hi.astype(F32)).astype(BF16)
    tri = jnp.where(lvl < len(GLA_LEVELS), 1.0, 0.0).astype(BF16)
    return jnp.dot(tri, g_hi, preferred_element_type=F32) + jnp.dot(tri, g_lo, preferred_element_type=F32)


def _gla_operands(qk, b, rev):
    c = GLA_CHUNK
    q = qk[:, :GLA_KW].astype(F32) * GLA_DK ** -0.5
    k = qk[:, GLA_KW:].astype(F32)
    b_total = b[0:1] if rev else b[c - 1:c]
    q_inter = (q * jnp.exp(b)).astype(BF16)
    k_state = (k * jnp.exp(b_total - b)).astype(BF16)
    decay_total = jnp.exp(b_total)

    row = lax.broadcasted_iota(jnp.int32, (c, GLA_KW), 0)
    pairs = []
    for size in GLA_LEVELS:
        n = c // size
        if size == GLA_LEVELS[0]:
            if rev:
                picks = [(i + 1) * size if i + 1 < n else None for i in range(n)]
            else:
                picks = [i * size - 1 if i > 0 else None for i in range(n)]
            d = _minus_rows(b, picks, size)
            q_hat = q * jnp.exp(d)
            k_hat = k * jnp.exp(jnp.minimum(-d, GLA_EXP_CLAMP))
        else:
            half = size // 2
            d = _minus_rows(b, [i * size + half - (0 if rev else 1) for i in range(n)], size)
            late = (row & half) != 0
            q_side = jnp.logical_not(late) if rev else late
            q_hat = q * jnp.exp(jnp.where(q_side, d, -jnp.inf))
            k_hat = k * jnp.exp(jnp.where(q_side, -jnp.inf, -d))
        pairs.append((q_hat.astype(BF16), k_hat.astype(BF16)))
    return q_inter, k_state, decay_total, pairs


def _gla_kernel(*refs):
    qkf_ref, vf_ref, lrf_ref, qkb_ref, vb_ref, lrb_ref, a_ref, bias_ref, lvl_ref, s0_ref = refs[:10]
    of_ref, ob_ref, so_ref, st_ref = refs[-4:]
    j = pl.program_id(1)

    @pl.when(j == 0)
    def _():
        for z in range(2):
            for h in range(GLA_HEADS):
                st_ref[z, h] = s0_ref[0, z, h].T

    n_chunks = qkf_ref.shape[0] // GLA_CHUNK
    chunk_rows = [slice(c * GLA_CHUNK, (c + 1) * GLA_CHUNK) for c in range(n_chunks)]
    units = [(z, rows) for rows_f, rows_b in zip(chunk_rows, reversed(chunk_rows))
             for z, rows in ((0, rows_f), (1, rows_b))]
    src = ((qkf_ref, vf_ref, lrf_ref, of_ref), (qkb_ref, vb_ref, lrb_ref, ob_ref))
    n_levels = len(GLA_LEVELS)
    hk = [slice(h * GLA_DK, (h + 1) * GLA_DK) for h in range(GLA_HEADS)]
    hv = [slice(h * GLA_DV, (h + 1) * GLA_DV) for h in range(GLA_HEADS)]
    tn = (((0,), (0,)), ((), ()))

    g = [_gla_decays(src[z][2][rows, :], a_ref[z], bias_ref[z]) for z, rows in units]
    b = [_gla_cumsum(gi, lvl_ref[z]) for gi, (z, rows) in zip(g, units)]
    v = [src[z][1][rows, :] for z, rows in units]
    state = [[st_ref[z, h] for h in range(GLA_HEADS)] for z in range(2)]
    ops, delta, scores = {}, {}, {}

    def queue_scores(u):
        z, rows = units[u]
        ops[u] = _gla_operands(src[z][0][rows, :], b[u], z == 1)
        delta[u] = [lax.dot_general(v[u][:, hv[h]], ops[u][1][:, hk[h]], tn, preferred_element_type=F32)
                    for h in range(GLA_HEADS)]
        scores[u] = [[lax.dot_general(qh[:, hk[h]], kh[:, hk[h]], _NT, preferred_element_type=F32)
                      for qh, kh in ops[u][3]] for h in range(GLA_HEADS)]

    def consume(u):
        z, rows = units[u]
        lvl = lvl_ref[z]
        q_inter, _, decay_total, _ = ops.pop(u)
        for h in range(GLA_HEADS):
            p = scores[u][h]
            att = p[-1]
            for i in reversed(range(n_levels - 1)):
                att = jnp.where(lvl == i, p[i], att)
            o = jnp.dot(att.astype(BF16), v[u][:, hv[h]], preferred_element_type=F32) + lax.dot_general(
                q_inter[:, hk[h]], state[z][h].astype(BF16), _NT, preferred_element_type=F32)
            src[z][3][rows, hv[h]] = o.astype(BF16)
            state[z][h] = state[z][h] * decay_total[:, hk[h]] + delta[u][h]

    for u in range(len(units)):
        queue_scores(u)
        if u > 0:
            consume(u - 1)
    consume(len(units) - 1)
    for z in range(2):
        for h in range(GLA_HEADS):
            st_ref[z, h] = state[z][h]

    @pl.when(j == pl.num_programs(1) - 1)
    def _():
        for z in range(2):
            for h in range(GLA_HEADS):
                so_ref[0, 0, z, h] = st_ref[z, h].T


def _gla(proj, lr, a_pad, bias, s0, batch, seq_len, states=None, layer=0, depth=1):
    t = proj.shape[0]
    blk = min(GLA_BLOCK, seq_len)
    nb = seq_len // blk
    fwd = lambda b, j: b * nb + j
    bwd = lambda b, j: b * nb + nb - 1 - j
    qk_w = 2 * GLA_KW

    def proj_specs(row):
        return [pl.BlockSpec((blk, qk_w), lambda b, j: (row(b, j), COL_GQK // qk_w)),
                pl.BlockSpec((blk, GLA_VW), lambda b, j: (row(b, j), COL_GV // GLA_VW)),
                pl.BlockSpec((blk, LR_BLOCK), lambda b, j: (row(b, j), 0))]

    state = (2, GLA_HEADS, GLA_DK, GLA_DV)
    in_specs = proj_specs(fwd) + proj_specs(bwd) + [
        pl.BlockSpec((2, LR_BLOCK, GLA_KW), lambda b, j: (0, 0, 0)),
        pl.BlockSpec((2, 1, GLA_KW), lambda b, j: (0, 0, 0)),
        pl.BlockSpec((2, GLA_CHUNK, GLA_CHUNK), lambda b, j: (0, 0, 0)),
        pl.BlockSpec((1,) + state, lambda b, j: (b, 0, 0, 0, 0))]
    args = [proj, proj, lr, proj, proj, lr, a_pad, bias, _gla_level_table(), s0]
    aliases = {}
    if states is not None:
        aliases = {len(args): 2}
        in_specs.append(pl.BlockSpec(memory_space=pl.ANY))
        args.append(states)
    return pl.pallas_call(
        _gla_kernel,
        grid=(batch, nb),
        in_specs=in_specs,
        out_specs=[pl.BlockSpec((blk, GLA_VW), lambda b, j: (fwd(b, j), 0)),
                   pl.BlockSpec((blk, GLA_VW), lambda b, j: (bwd(b, j), 0)),
                   pl.BlockSpec((1, 1) + state, lambda b, j: (b, layer, 0, 0, 0, 0))],
        out_shape=[jax.ShapeDtypeStruct((t, GLA_VW), BF16),
                   jax.ShapeDtypeStruct((t, GLA_VW), BF16),
                   jax.ShapeDtypeStruct((batch, depth) + state, F32)],
        scratch_shapes=[pltpu.VMEM((2, GLA_HEADS, GLA_DV, GLA_DK), F32)],
        input_output_aliases=aliases,
        compiler_params=pltpu.CompilerParams(dimension_semantics=("parallel", "arbitrary"),
                                             vmem_limit_bytes=VMEM_LIMIT),
        name="gla",
    )(*args)


HALO = 16


def _out_kernel(*refs, tm, seq_len, final):
    (x_ref, a_ref, ch_ref, cb_ref, cc_ref, cg_ref, chp_ref, ccp_ref, chn_ref, ccn_ref,
     gg_ref, of_ref, ob_ref, cw_ref, gnw_ref, gate_ref, w_ref) = refs[:17]
    pos = 17
    if final:
        fnw_ref = refs[pos]
        pos += 1
    o_ref = refs[pos]

    def prod(c_ref, h_ref, r):
        return c_ref[r:r + 1, :].astype(F32) * h_ref[r:r + 1, :].astype(F32)

    u = cc_ref[...].astype(F32) * ch_ref[...].astype(F32)
    row = lax.broadcasted_iota(jnp.int32, u.shape, 0)
    seq_pos = (pl.program_id(0) * tm + row) & (seq_len - 1)
    u_prev = jnp.where(row == 0, prod(ccp_ref, chp_ref, HALO - 1), pltpu.roll(u, 1, axis=0))
    u_prev = jnp.where(seq_pos == 0, 0.0, u_prev)
    u_next = jnp.where(row == tm - 1, prod(ccn_ref, chn_ref, 0), pltpu.roll(u, tm - 1, axis=0))
    u_next = jnp.where(seq_pos == seq_len - 1, 0.0, u_next)
    conv = u_prev * cw_ref[0:1, :] + u * cw_ref[1:2, :] + u_next * cw_ref[2:3, :]
    out_b = cb_ref[...].astype(F32) * conv * _silu(cg_ref[...].astype(F32))
    parts = [a_ref[...], out_b.astype(BF16)]

    o = of_ref[...].astype(F32) + ob_ref[...].astype(F32)
    gg = gg_ref[...].astype(F32)
    for h in range(GLA_HEADS):
        cols = slice(h * GLA_DV, (h + 1) * GLA_DV)
        parts.append((_rms(o[:, cols], gnw_ref[...]) * _silu(gg[:, cols])).astype(BF16))

    r = jnp.dot(jnp.concatenate(parts, axis=1), w_ref[0], preferred_element_type=F32)
    xn = x_ref[...] + gate_ref[0] * r
    if final:
        xn = _rms(xn, fnw_ref[...])
    o_ref[...] = xn


def _out_projection(x, att, proj, o_f, o_b, conv_w, gla_norm_w, mod3, w_out, layer, final_norm_w, mod_row,
                    seq_len):
    t, d = x.shape
    tm = 512
    cw = conv_w.shape[1]
    final = final_norm_w is not None
    per_halo = tm // HALO
    last_halo = t // HALO - 1

    def col(c, width):
        return lambda i: (i, c // width)

    prev = lambda c: (lambda i: (jnp.maximum(i * per_halo - 1, 0), c // cw))
    nxt = lambda c: (lambda i: (jnp.minimum((i + 1) * per_halo, last_halo), c // cw))
    in_specs = [pl.BlockSpec((tm, d), lambda i: (i, 0)),
                pl.BlockSpec((tm, ATT_WIDTH), lambda i: (i, 0)),
                pl.BlockSpec((tm, cw), col(COL_CH, cw)),
                pl.BlockSpec((tm, cw), col(COL_CB, cw)),
                pl.BlockSpec((tm, cw), col(COL_CC, cw)),
                pl.BlockSpec((tm, cw), col(COL_CG, cw)),
                pl.BlockSpec((HALO, cw), prev(COL_CH)),
                pl.BlockSpec((HALO, cw), prev(COL_CC)),
                pl.BlockSpec((HALO, cw), nxt(COL_CH)),
                pl.BlockSpec((HALO, cw), nxt(COL_CC)),
                pl.BlockSpec((tm, GLA_VW), col(COL_GG, GLA_VW)),
                pl.BlockSpec((tm, GLA_VW), lambda i: (i, 0)),
                pl.BlockSpec((tm, GLA_VW), lambda i: (i, 0)),
                pl.BlockSpec((3, cw), lambda i: (0, 0)),
                pl.BlockSpec((1, GLA_DV), lambda i: (0, 0)),
                pl.BlockSpec((1, 1, d), lambda i: (mod_row(i * tm), 0, 2)),
                pl.BlockSpec((1,) + w_out.shape[1:], lambda i: (layer, 0, 0))]
    args = [x, att, proj, proj, proj, proj, proj, proj, proj, proj, proj, o_f, o_b,
            conv_w, gla_norm_w.reshape(1, GLA_DV), mod3, w_out]
    if final:
        in_specs.append(pl.BlockSpec((1, d), lambda i: (0, 0)))
        args.append(final_norm_w.reshape(1, d))
    return pl.pallas_call(
        functools.partial(_out_kernel, tm=tm, seq_len=seq_len, final=final),
        grid=(t // tm,),
        in_specs=in_specs,
        out_specs=pl.BlockSpec((tm, d), lambda i: (i, 0)),
        out_shape=jax.ShapeDtypeStruct((t, d), F32),
        compiler_params=pltpu.CompilerParams(dimension_semantics=("parallel",),
                                             vmem_limit_bytes=VMEM_LIMIT),
        name="out_projection",
    )(*args)


def _rope_tables(seq_len):
    pos = jnp.arange(seq_len)
    row = (pos // GRID_W).astype(F32)
    col = (pos % GRID_W).astype(F32)
    n_freq = HEAD_DIM // 4
    inv = ROPE_THETA ** (-jnp.arange(n_freq, dtype=F32) / n_freq)
    ang_r = row[:, None] * inv[None, :]
    ang_c = col[:, None] * inv[None, :]
    cos = jnp.concatenate([jnp.cos(ang_r), jnp.cos(ang_r), jnp.cos(ang_c), jnp.cos(ang_c)], axis=-1)
    sin = jnp.concatenate([-jnp.sin(ang_r), jnp.sin(ang_r), -jnp.sin(ang_c), jnp.sin(ang_c)], axis=-1)
    return cos, sin


@jax.jit
def _forward(x_prompt, x_sample, cache_k, cache_v, state_gla, c, c_ctx, w_mod, b_mod, norm_w, w_in,
             q_norm_w, k_norm_w, conv_w, gla_a_up, gla_a_bias, gla_norm_w, w_out, final_norm_w):
    batch, seq, d = x_prompt.shape
    dec_batch, dec_seq, _ = x_sample.shape
    depth = w_in.shape[0]
    past = cache_k.shape[2]

    cond = jnp.zeros((8, d), F32).at[0].set(c_ctx).at[1:1 + dec_batch].set(c)
    mod3 = _modulation(cond, w_mod, b_mod).reshape(depth * 8, 1, 3 * d)
    rope_tables = _rope_tables(dec_seq)

    h = x_prompt.reshape(batch * seq, d)
    z = x_sample.reshape(dec_batch * dec_seq, d)
    zero_state = jnp.zeros((batch, 2, GLA_HEADS, GLA_DK, GLA_DV), F32)
    w_in_t = jnp.swapaxes(w_in, 1, 2).astype(BF16)
    w_lr_t = jnp.pad(w_in_t[:, COL_LR:, :], ((0, 0), (0, LR_BLOCK - (IN_WIDTH - COL_LR)), (0, 0)))
    w_out_b = w_out.astype(BF16)
    new_kv, new_state = None, None
    for l in range(depth):
        a_pad = jnp.zeros((2, LR_BLOCK, GLA_KW), F32)
        for zdir in range(2):
            a_pad = a_pad.at[zdir, zdir * GLA_RANK:(zdir + 1) * GLA_RANK].set(gla_a_up[l, zdir])
        a_pad = a_pad.astype(BF16)
        bias = gla_a_bias[l].reshape(2, 1, GLA_KW)
        final_w = final_norm_w if l == depth - 1 else None

        row_ctx = lambda tok, l=l: l * 8
        proj, lr = _in_projection(h, mod3, norm_w[l], w_in_t, w_lr_t, l, row_ctx)
        att, new_kv = _context_attention(proj, q_norm_w[l], k_norm_w[l], batch, seq, new_kv, l, depth)
        o_f, o_b, new_state = _gla(proj, lr, a_pad, bias, zero_state, batch, seq, new_state, l, depth)
        h = _out_projection(h, att, proj, o_f, o_b, conv_w[l], gla_norm_w[l], mod3, w_out_b, l, final_w,
                            row_ctx, seq)

        row_lat = lambda tok, l=l: l * 8 + 1 + tok // dec_seq
        proj, lr = _in_projection(z, mod3, norm_w[l], w_in_t, w_lr_t, l, row_lat)
        ctx_kv = (cache_k[:, l].reshape(dec_batch * past, KV_WIDTH),
                  cache_v[:, l].reshape(dec_batch * past, KV_WIDTH))
        k_all, vt_all, k_norm2 = _kv_prepare(proj, k_norm_w[l], rope_tables, ctx_kv, dec_batch, dec_seq)
        att = _attention(proj, k_all, vt_all, k_norm2, q_norm_w[l], rope_tables, dec_batch, dec_seq)
        o_f, o_b, _ = _gla(proj, lr, a_pad, bias, state_gla[:, l], dec_batch, dec_seq)
        z = _out_projection(z, att, proj, o_f, o_b, conv_w[l], gla_norm_w[l], mod3, w_out_b, l, final_w,
                            row_lat, dec_seq)

    kv_shape = (batch, depth, seq, ATT_KV_HEADS, HEAD_DIM)
    return (h.reshape(batch, seq, d), z.reshape(dec_batch, dec_seq, d),
            new_kv[0].reshape(kv_shape), new_kv[1].reshape(kv_shape), new_state)


def kernel(x_prompt, x_sample, cache_k, cache_v, state_gla, c, c_ctx, w_mod, b_mod, norm_w, w_in, q_norm_w,
           k_norm_w, conv_w, gla_a_up, gla_a_bias, gla_norm_w, w_out, final_norm_w):
    return _forward(x_prompt, x_sample, cache_k, cache_v, state_gla, c, c_ctx, w_mod, b_mod, norm_w, w_in,
                    q_norm_w, k_norm_w, conv_w, gla_a_up, gla_a_bias, gla_norm_w, w_out, final_norm_w)
```

```python
import functools

import numpy as np
import jax
import jax.numpy as jnp
from jax import lax
from jax.experimental import pallas as pl
from jax.experimental.pallas import tpu as pltpu

F32 = jnp.float32
BF16 = jnp.bfloat16

GRID_W = 64
HEAD_DIM = 128
ATT_HEADS = 8
ATT_KV_HEADS = 2
HEADS_PER_KV = ATT_HEADS // ATT_KV_HEADS
ATT_WIDTH = ATT_HEADS * HEAD_DIM
KV_WIDTH = ATT_KV_HEADS * HEAD_DIM
GLA_HEADS = 4
GLA_DK = 64
GLA_DV = 128
GLA_KW = GLA_HEADS * GLA_DK
GLA_VW = GLA_HEADS * GLA_DV
GLA_RANK = 16
GLA_TAU = 16.0
ROPE_THETA = 10000.0
LOG2_E = 1.4426950408889634
ATT_DIRECT_EXP_LIMIT = 48.0
ATT_LOOKAHEAD = 2
CTX_LOOKAHEAD = 4
EPS = 1e-6

COL_Q = 0
COL_K = 1024
COL_V = 1280
COL_AG = 1536
COL_CH = 2560
COL_CB = 3072
COL_CC = 3584
COL_CG = 4096
COL_GQK = 4608
COL_GV = 5120
COL_GG = 5632
COL_LR = 6144
IN_WIDTH = 6176
IN_TILE = 2048
IN_ROW_CHUNKS = 4
IN_MAIN = COL_LR
LR_BLOCK = 128

GLA_CHUNK = 128
GLA_LEVELS = (32, 64, 128)
GLA_BLOCK = 256
GLA_EXP_CLAMP = 60.0

VMEM_LIMIT = 56 * 1024 * 1024


def _silu(x):
    return x * jax.nn.sigmoid(x)


def _rms(x, w):
    return x * lax.rsqrt(jnp.mean(x * x, axis=-1, keepdims=True) + EPS) * w


def _rope(y, cos, sin_signed):
    lane = lax.broadcasted_iota(jnp.int32, y.shape, 1)
    partner = jnp.where((lane & 63) < 32, pltpu.roll(y, 96, axis=1), pltpu.roll(y, 32, axis=1))
    return y * cos + partner * sin_signed


def _mod_kernel(c_ref, w_ref, b_ref, o_ref):
    a = _silu(c_ref[...]).astype(BF16)
    o_ref[0] = jnp.dot(a, w_ref[0].astype(BF16), preferred_element_type=F32) + b_ref[0]


def _modulation(cond, w_mod, b_mod):
    depth, d, n = w_mod.shape
    tn = 768
    return pl.pallas_call(
        _mod_kernel,
        grid=(depth, n // tn),
        in_specs=[pl.BlockSpec((8, d), lambda l, j: (0, 0)),
                  pl.BlockSpec((1, d, tn), lambda l, j: (l, 0, j)),
                  pl.BlockSpec((1, 1, tn), lambda l, j: (l, 0, j))],
        out_specs=pl.BlockSpec((1, 8, tn), lambda l, j: (l, 0, j)),
        out_shape=jax.ShapeDtypeStruct((depth, 8, n), F32),
        compiler_params=pltpu.CompilerParams(dimension_semantics=("parallel", "parallel"),
                                             vmem_limit_bytes=VMEM_LIMIT),
        name="modulation",
    )(cond, w_mod, b_mod.reshape(depth, 1, n))


_NT = (((1,), (1,)), ((), ()))


def _in_kernel(x_ref, nw_ref, shift_ref, scale_ref, w_ref, wlr_ref, o_ref, lr_ref, xn_ref):
    def project(xn, w):
        return lax.dot_general(xn, w, _NT, preferred_element_type=F32).astype(BF16)

    @pl.when(pl.program_id(1) == 0)
    def _():
        gain = nw_ref[...] * (1.0 + scale_ref[0])
        per = x_ref.shape[0] // IN_ROW_CHUNKS
        for r in range(IN_ROW_CHUNKS):
            rows = slice(r * per, (r + 1) * per)
            x = x_ref[rows, :]
            inv = lax.rsqrt(jnp.mean(x * x, axis=-1, keepdims=True) + EPS)
            xn = (x * inv * gain + shift_ref[0]).astype(BF16)
            xn_ref[rows, :] = xn
            o_ref[rows, :] = project(xn, w_ref[0])
            lr_ref[rows, :] = project(xn, wlr_ref[0])

    @pl.when(pl.program_id(1) > 0)
    def _():
        o_ref[...] = project(xn_ref[...], w_ref[0])


def _in_projection(x, mod3, norm_w, w_in_t, w_lr_t, layer, mod_row):
    t, d = x.shape
    tm = 1024
    return pl.pallas_call(
        _in_kernel,
        grid=(t // tm, IN_MAIN // IN_TILE),
        in_specs=[pl.BlockSpec((tm, d), lambda i, j: (i, 0)),
                  pl.BlockSpec((1, d), lambda i, j: (0, 0)),
                  pl.BlockSpec((1, 1, d), lambda i, j: (mod_row(i * tm), 0, 0)),
                  pl.BlockSpec((1, 1, d), lambda i, j: (mod_row(i * tm), 0, 1)),
                  pl.BlockSpec((1, IN_TILE, d), lambda i, j: (layer, j, 0)),
                  pl.BlockSpec((1, LR_BLOCK, d), lambda i, j: (layer, 0, 0))],
        out_specs=[pl.BlockSpec((tm, IN_TILE), lambda i, j: (i, j)),
                   pl.BlockSpec((tm, LR_BLOCK), lambda i, j: (i, 0))],
        out_shape=[jax.ShapeDtypeStruct((t, IN_MAIN), BF16),
                   jax.ShapeDtypeStruct((t, LR_BLOCK), BF16)],
        scratch_shapes=[pltpu.VMEM((tm, d), BF16)],
        compiler_params=pltpu.CompilerParams(dimension_semantics=("parallel", "arbitrary"),
                                             vmem_limit_bytes=VMEM_LIMIT),
        name="in_projection",
    )(x, norm_w.reshape(1, d), mod3, mod3, w_in_t, w_lr_t)


def _kv_kernel(kv_ref, kw_ref, cos_ref, sin_ref, ck_ref, cv_ref, k_ref, vt_ref, kmax_ref, *, n_self):
    @pl.when(pl.program_id(1) == 0)
    def _():
        kmax_ref[...] = jnp.zeros(kmax_ref.shape, F32)

    def put_keys(g, kb):
        cols = slice(g * HEAD_DIM, (g + 1) * HEAD_DIM)
        k_ref[:, cols] = kb
        kf = kb.astype(F32)
        n2 = jnp.max(jnp.sum(kf * kf, axis=1, keepdims=True), axis=0, keepdims=True)
        kmax_ref[0, g:g + 1, :] = jnp.maximum(kmax_ref[0, g:g + 1, :], jnp.broadcast_to(n2, (1, HEAD_DIM)))

    @pl.when(pl.program_id(1) < n_self)
    def _():
        for g in range(ATT_KV_HEADS):
            cols = slice(g * HEAD_DIM, (g + 1) * HEAD_DIM)
            y = _rope(_rms(kv_ref[:, cols].astype(F32), kw_ref[...]), cos_ref[...], sin_ref[...])
            put_keys(g, y.astype(BF16))
        vt_ref[...] = kv_ref[:, KV_WIDTH:].astype(F32).T.astype(BF16)

    @pl.when(pl.program_id(1) >= n_self)
    def _():
        for g in range(ATT_KV_HEADS):
            put_keys(g, ck_ref[:, g * HEAD_DIM:(g + 1) * HEAD_DIM].astype(BF16))
        vt_ref[...] = cv_ref[...].T.astype(BF16)


def _kv_prepare(proj, k_norm_w, rope_tables, ctx_kv, batch, seq_len):
    tr = 512
    n_self = seq_len // tr
    n_ctx = ctx_kv[0].shape[0] // batch // tr
    n_all = n_self + n_ctx
    own = lambda j: jnp.minimum(j, n_self - 1)
    ctx_spec = pl.BlockSpec((tr, KV_WIDTH), lambda b, j: (b * n_ctx + jnp.maximum(j - n_self, 0), 0))
    rope_spec = pl.BlockSpec((tr, HEAD_DIM), lambda b, j: (own(j), 0))
    return pl.pallas_call(
        functools.partial(_kv_kernel, n_self=n_self),
        grid=(batch, n_all),
        in_specs=[pl.BlockSpec((tr, 2 * KV_WIDTH), lambda b, j: (b * n_self + own(j), COL_K // (2 * KV_WIDTH))),
                  pl.BlockSpec((1, HEAD_DIM), lambda b, j: (0, 0)),
                  rope_spec, rope_spec, ctx_spec, ctx_spec],
        out_specs=[pl.BlockSpec((tr, KV_WIDTH), lambda b, j: (b * n_all + j, 0)),
                   pl.BlockSpec((KV_WIDTH, tr), lambda b, j: (b, j)),
                   pl.BlockSpec((1, 8, HEAD_DIM), lambda b, j: (b, 0, 0))],
        out_shape=[jax.ShapeDtypeStruct((batch * n_all * tr, KV_WIDTH), BF16),
                   jax.ShapeDtypeStruct((batch * KV_WIDTH, n_all * tr), BF16),
                   jax.ShapeDtypeStruct((batch, 8, HEAD_DIM), F32)],
        compiler_params=pltpu.CompilerParams(dimension_semantics=("parallel", "arbitrary")),
        name="kv_prepare",
    )(proj, k_norm_w.reshape(1, HEAD_DIM), *rope_tables, *ctx_kv)


def _attn_kernel(q_ref, gate_ref, k_ref, vt_ref, kmax_ref, qw_ref, cos_ref, sin_ref,
                 o_ref, qt_ref, s0_ref, s1_ref, m_ref, l_ref, acc_ref, *, tq, tk, n_chunks):
    q_scale = HEAD_DIM ** -0.5 * LOG2_E
    heads = [slice(h * tq, (h + 1) * tq) for h in range(HEADS_PER_KV)]
    dims = [slice(h * HEAD_DIM, (h + 1) * HEAD_DIM) for h in range(HEADS_PER_KV)]

    def rows(c):
        off = c * tk
        return pl.ds(off if isinstance(off, int) else pl.multiple_of(off, tk), tk)

    def prepare(h):
        y = _rope(_rms(q_ref[:, dims[h]].astype(F32), qw_ref[...]), cos_ref[...], sin_ref[...])
        qt_ref[:, heads[h]] = (y * q_scale).T.astype(BF16)
        l_ref[:, heads[h]] = jnp.zeros((1, tq), F32)
        acc_ref[:, heads[h]] = jnp.zeros((HEAD_DIM, tq), F32)

    def finish(h):
        o = (acc_ref[:, heads[h]] / l_ref[:, heads[h]]).T
        o_ref[:, dims[h]] = (o * _silu(gate_ref[:, dims[h]].astype(F32))).astype(BF16)

    w = qw_ref[...]
    q_norm2 = HEAD_DIM * q_scale * q_scale * jnp.max(w * w)
    key_norm2 = jnp.max(kmax_ref[0, pl.ds(pl.program_id(1), 1), :])
    bounded = q_norm2 * key_norm2 < ATT_DIRECT_EXP_LIMIT ** 2

    @pl.when(bounded)
    def _():
        pieces = [(h, c) for h in range(HEADS_PER_KV) for c in range(n_chunks)]
        pending = {}
        for idx in range(len(pieces) + ATT_LOOKAHEAD):
            if idx < len(pieces):
                h, c = pieces[idx]
                if c == 0:
                    prepare(h)
                pending[idx] = jnp.dot(k_ref[rows(c), :], qt_ref[:, heads[h]], preferred_element_type=F32)
            if idx >= ATT_LOOKAHEAD:
                h, c = pieces[idx - ATT_LOOKAHEAD]
                p = jnp.exp2(pending.pop(idx - ATT_LOOKAHEAD))
                l_ref[:, heads[h]] += jnp.sum(p, axis=0, keepdims=True)
                acc_ref[:, heads[h]] += jnp.dot(vt_ref[:, rows(c)], p.astype(BF16), preferred_element_type=F32)
                if c == n_chunks - 1:
                    finish(h)

    @pl.when(jnp.logical_not(bounded))
    def _():
        for h in range(HEADS_PER_KV):
            prepare(h)
        m_ref[...] = jnp.full(m_ref.shape, -jnp.inf, F32)

        def scores(c, s_ref):
            s_ref[...] = jnp.dot(k_ref[rows(c), :], qt_ref[...], preferred_element_type=F32)

        def update(c, s_ref):
            s = s_ref[...]
            m_old = m_ref[...]
            m_new = jnp.maximum(m_old, jnp.max(s, axis=0, keepdims=True))
            p = jnp.exp2(s - m_new)
            alpha = jnp.exp2(m_old - m_new)
            l_ref[...] = alpha * l_ref[...] + jnp.sum(p, axis=0, keepdims=True)
            pv = jnp.dot(vt_ref[:, rows(c)], p.astype(BF16), preferred_element_type=F32)
            acc_ref[...] = alpha * acc_ref[...] + pv
            m_ref[...] = m_new

        scores(0, s0_ref)
        n_pairs = (n_chunks - 1) // 2

        def body(i, carry):
            c = 2 * i
            scores(c + 1, s1_ref)
            update(c, s0_ref)
            scores(c + 2, s0_ref)
            update(c + 1, s1_ref)
            return carry

        lax.fori_loop(0, n_pairs, body, 0)
        done = 2 * n_pairs
        if n_chunks - done == 2:
            scores(done + 1, s1_ref)
            update(done, s0_ref)
            update(done + 1, s1_ref)
        else:
            update(done, s0_ref)
        for h in range(HEADS_PER_KV):
            finish(h)


def _attention(proj, k_all, vt_all, k_norm2, q_norm_w, rope_tables, batch, seq_len):
    t = proj.shape[0]
    keys = k_all.shape[0] // batch
    tq = 512
    tk = min(512, keys)
    nq = seq_len // tq
    gw = HEADS_PER_KV * HEAD_DIM
    rope_spec = pl.BlockSpec((tq, HEAD_DIM), lambda b, g, i: (i, 0))
    in_specs = [pl.BlockSpec((tq, gw), lambda b, g, i: (b * nq + i, COL_Q // gw + g)),
                pl.BlockSpec((tq, gw), lambda b, g, i: (b * nq + i, COL_AG // gw + g)),
                pl.BlockSpec((keys, HEAD_DIM), lambda b, g, i: (b, g)),
                pl.BlockSpec((HEAD_DIM, keys), lambda b, g, i: (b * ATT_KV_HEADS + g, 0)),
                pl.BlockSpec((1,) + k_norm2.shape[1:], lambda b, g, i: (b, 0, 0)),
                pl.BlockSpec((1, HEAD_DIM), lambda b, g, i: (0, 0)),
                rope_spec, rope_spec]
    args = [proj, proj, k_all, vt_all, k_norm2, q_norm_w.reshape(1, HEAD_DIM), *rope_tables]
    return pl.pallas_call(
        functools.partial(_attn_kernel, tq=tq, tk=tk, n_chunks=keys // tk),
        grid=(batch, ATT_KV_HEADS, nq),
        in_specs=in_specs,
        out_specs=pl.BlockSpec((tq, gw), lambda b, g, i: (b * nq + i, g)),
        out_shape=jax.ShapeDtypeStruct((t, ATT_WIDTH), BF16),
        scratch_shapes=[pltpu.VMEM((HEAD_DIM, HEADS_PER_KV * tq), BF16),
                        pltpu.VMEM((tk, HEADS_PER_KV * tq), F32),
                        pltpu.VMEM((tk, HEADS_PER_KV * tq), F32),
                        pltpu.VMEM((1, HEADS_PER_KV * tq), F32),
                        pltpu.VMEM((1, HEADS_PER_KV * tq), F32),
                        pltpu.VMEM((HEAD_DIM, HEADS_PER_KV * tq), F32)],
        compiler_params=pltpu.CompilerParams(dimension_semantics=("parallel", "parallel", "arbitrary"),
                                             vmem_limit_bytes=VMEM_LIMIT),
        name="attention",
    )(*args)


def _ctx_attn_kernel(*refs):
    q_ref, kv_ref, gate0_ref, gate1_ref, qw_ref, kw_ref = refs[:6]
    o_ref, kn_ref, vf_ref = refs[-3:]
    seq = q_ref.shape[0]
    q_scale = HEAD_DIM ** -0.5 * LOG2_E
    gates = (gate0_ref, gate1_ref)
    keys, values_t = [], []
    for g in range(ATT_KV_HEADS):
        dims = slice(g * HEAD_DIM, (g + 1) * HEAD_DIM)
        kn = _rms(kv_ref[:, dims].astype(F32), kw_ref[...])
        kn_ref[pl.ds(g, seq, stride=ATT_KV_HEADS), :] = kn
        keys.append(kn.astype(BF16))
        v = kv_ref[:, KV_WIDTH + g * HEAD_DIM:KV_WIDTH + (g + 1) * HEAD_DIM].astype(F32)
        vf_ref[pl.ds(g, seq, stride=ATT_KV_HEADS), :] = v
        values_t.append(v.T.astype(BF16))

    pending = {}
    for idx in range(ATT_HEADS + CTX_LOOKAHEAD):
        if idx < ATT_HEADS:
            dims = slice(idx * HEAD_DIM, (idx + 1) * HEAD_DIM)
            y = _rms(q_ref[:, dims].astype(F32), qw_ref[...]) * q_scale
            pending[idx] = jnp.dot(keys[idx // HEADS_PER_KV], y.T.astype(BF16), preferred_element_type=F32)
        head = idx - CTX_LOOKAHEAD
        if head >= 0:
            g, h = divmod(head, HEADS_PER_KV)
            s = pending.pop(head)
            p = jnp.exp2(s - jnp.max(s, axis=0, keepdims=True))
            l = jnp.sum(p, axis=0, keepdims=True)
            acc = jnp.dot(values_t[g], p.astype(BF16), preferred_element_type=F32)
            gate = gates[g][:, h * HEAD_DIM:(h + 1) * HEAD_DIM].astype(F32)
            o_ref[:, head * HEAD_DIM:(head + 1) * HEAD_DIM] = ((acc / l).T * _silu(gate)).astype(BF16)


def _context_attention(proj, q_norm_w, k_norm_w, batch, seq_len, collected, layer, depth):
    t = proj.shape[0]
    gw = HEADS_PER_KV * HEAD_DIM
    row_block = lambda width, col: pl.BlockSpec((seq_len, width), lambda b: (b, col // width))
    vec = pl.BlockSpec((1, HEAD_DIM), lambda b: (0, 0))
    in_specs = [row_block(ATT_WIDTH, COL_Q), row_block(2 * KV_WIDTH, COL_K),
                row_block(gw, COL_AG), row_block(gw, COL_AG + gw), vec, vec]
    args = [proj, proj, proj, proj, q_norm_w.reshape(1, HEAD_DIM), k_norm_w.reshape(1, HEAD_DIM)]
    aliases = {}
    if collected is not None:
        aliases = {len(args): 1, len(args) + 1: 2}
        in_specs += [pl.BlockSpec(memory_space=pl.ANY)] * 2
        args += list(collected)
    rows = seq_len * ATT_KV_HEADS
    slot = pl.BlockSpec((rows, HEAD_DIM), lambda b: (b * depth + layer, 0))
    att, keys, values = pl.pallas_call(
        _ctx_attn_kernel,
        grid=(batch,),
        in_specs=in_specs,
        out_specs=[row_block(ATT_WIDTH, 0), slot, slot],
        out_shape=[jax.ShapeDtypeStruct((t, ATT_WIDTH), BF16),
                   jax.ShapeDtypeStruct((batch * depth * rows, HEAD_DIM), F32),
                   jax.ShapeDtypeStruct((batch * depth * rows, HEAD_DIM), F32)],
        input_output_aliases=aliases,
        compiler_params=pltpu.CompilerParams(dimension_semantics=("parallel",),
                                             vmem_limit_bytes=VMEM_LIMIT),
        name="context_attention",
    )(*args)
    return att, (keys, values)


def _log_sigmoid(x):
    return -(jnp.maximum(-x, 0.0) + jnp.log(1.0 + jnp.exp(-jnp.abs(x))))


def _gla_level_table():
    t = np.arange(GLA_CHUNK)[:, None]
    s = np.arange(GLA_CHUNK)[None, :]
    lvl = np.full((GLA_CHUNK, GLA_CHUNK), len(GLA_LEVELS), np.int32)
    for i, size in reversed(list(enumerate(GLA_LEVELS))):
        lvl = np.where(t // size == s // size, i, lvl)
    fwd = np.where(s <= t, lvl, len(GLA_LEVELS)).astype(np.int32)
    return jnp.asarray(np.stack([fwd, fwd.T]))


def _minus_rows(b, picks, size):
    return jnp.concatenate(
        [b[i * size:(i + 1) * size] if p is None else b[i * size:(i + 1) * size] - b[p:p + 1]
         for i, p in enumerate(picks)], axis=0)


def _gla_decays(lr, a, bias):
    logits = jnp.dot(lr, a, preferred_element_type=F32) + bias
    return _log_sigmoid(logits) * (1.0 / GLA_TAU)


def _gla_cumsum(g, lvl):
    g_hi = g.astype(BF16)
    g_lo = (g - g_hi.astype(F32)).astype(BF16)
    tri = jnp.where(lvl < len(GLA_LEVELS), 1.0, 0.0).astype(BF16)
    return jnp.dot(tri, g_hi, preferred_element_type=F32) + jnp.dot(tri, g_lo, preferred_element_type=F32)


def _gla_operands(qk, b, rev):
    c = GLA_CHUNK
    q = qk[:, :GLA_KW].astype(F32) * GLA_DK ** -0.5
    k = qk[:, GLA_KW:].astype(F32)
    b_total = b[0:1] if rev else b[c - 1:c]
    q_inter = (q * jnp.exp(b)).astype(BF16)
    k_state = (k * jnp.exp(b_total - b)).astype(BF16)
    decay_total = jnp.exp(b_total)

    row = lax.broadcasted_iota(jnp.int32, (c, GLA_KW), 0)
    pairs = []
    for size in GLA_LEVELS:
        n = c // size
        if size == GLA_LEVELS[0]:
            if rev:
                picks = [(i + 1) * size if i + 1 < n else None for i in range(n)]
            else:
                picks = [i * size - 1 if i > 0 else None for i in range(n)]
            d = _minus_rows(b, picks, size)
            q_hat = q * jnp.exp(d)
            k_hat = k * jnp.exp(jnp.minimum(-d, GLA_EXP_CLAMP))
        else:
            half = size // 2
            d = _minus_rows(b, [i * size + half - (0 if rev else 1) for i in range(n)], size)
            late = (row & half) != 0
            q_side = jnp.logical_not(late) if rev else late
            q_hat = q * jnp.exp(jnp.where(q_side, d, -jnp.inf))
            k_hat = k * jnp.exp(jnp.where(q_side, -jnp.inf, -d))
        pairs.append((q_hat.astype(BF16), k_hat.astype(BF16)))
    return q_inter, k_state, decay_total, pairs


def _gla_kernel(*refs):
    qkf_ref, vf_ref, lrf_ref, qkb_ref, vb_ref, lrb_ref, a_ref, bias_ref, lvl_ref, s0_ref = refs[:10]
    of_ref, ob_ref, so_ref, st_ref = refs[-4:]
    j = pl.program_id(1)

    @pl.when(j == 0)
    def _():
        for z in range(2):
            for h in range(GLA_HEADS):
                st_ref[z, h] = s0_ref[0, z, h].T

    n_chunks = qkf_ref.shape[0] // GLA_CHUNK
    chunk_rows = [slice(c * GLA_CHUNK, (c + 1) * GLA_CHUNK) for c in range(n_chunks)]
    units = [(z, rows) for rows_f, rows_b in zip(chunk_rows, reversed(chunk_rows))
             for z, rows in ((0, rows_f), (1, rows_b))]
    src = ((qkf_ref, vf_ref, lrf_ref, of_ref), (qkb_ref, vb_ref, lrb_ref, ob_ref))
    n_levels = len(GLA_LEVELS)
    hk = [slice(h * GLA_DK, (h + 1) * GLA_DK) for h in range(GLA_HEADS)]
    hv = [slice(h * GLA_DV, (h + 1) * GLA_DV) for h in range(GLA_HEADS)]
    tn = (((0,), (0,)), ((), ()))

    g = [_gla_decays(src[z][2][rows, :], a_ref[z], bias_ref[z]) for z, rows in units]
    b = [_gla_cumsum(gi, lvl_ref[z]) for gi, (z, rows) in zip(g, units)]
    v = [src[z][1][rows, :] for z, rows in units]
    state = [[st_ref[z, h] for h in range(GLA_HEADS)] for z in range(2)]
    ops, delta, scores = {}, {}, {}

    def queue_scores(u):
        z, rows = units[u]
        ops[u] = _gla_operands(src[z][0][rows, :], b[u], z == 1)
        delta[u] = [lax.dot_general(v[u][:, hv[h]], ops[u][1][:, hk[h]], tn, preferred_element_type=F32)
                    for h in range(GLA_HEADS)]
        scores[u] = [[lax.dot_general(qh[:, hk[h]], kh[:, hk[h]], _NT, preferred_element_type=F32)
                      for qh, kh in ops[u][3]] for h in range(GLA_HEADS)]

    def consume(u):
        z, rows = units[u]
        lvl = lvl_ref[z]
        q_inter, _, decay_total, _ = ops.pop(u)
        for h in range(GLA_HEADS):
            p = scores[u][h]
            att = p[-1]
            for i in reversed(range(n_levels - 1)):
                att = jnp.where(lvl == i, p[i], att)
            o = jnp.dot(att.astype(BF16), v[u][:, hv[h]], preferred_element_type=F32) + lax.dot_general(
                q_inter[:, hk[h]], state[z][h].astype(BF16), _NT, preferred_element_type=F32)
            src[z][3][rows, hv[h]] = o.astype(BF16)
            state[z][h] = state[z][h] * decay_total[:, hk[h]] + delta[u][h]

    for u in range(len(units)):
        queue_scores(u)
        if u > 0:
            consume(u - 1)
    consume(len(units) - 1)
    for z in range(2):
        for h in range(GLA_HEADS):
            st_ref[z, h] = state[z][h]

    @pl.when(j == pl.num_programs(1) - 1)
    def _():
        for z in range(2):
            for h in range(GLA_HEADS):
                so_ref[0, 0, z, h] = st_ref[z, h].T


def _gla(proj, lr, a_pad, bias, s0, batch, seq_len, states=None, layer=0, depth=1):
    t = proj.shape[0]
    blk = min(GLA_BLOCK, seq_len)
    nb = seq_len // blk
    fwd = lambda b, j: b * nb + j
    bwd = lambda b, j: b * nb + nb - 1 - j
    qk_w = 2 * GLA_KW

    def proj_specs(row):
        return [pl.BlockSpec((blk, qk_w), lambda b, j: (row(b, j), COL_GQK // qk_w)),
                pl.BlockSpec((blk, GLA_VW), lambda b, j: (row(b, j), COL_GV // GLA_VW)),
                pl.BlockSpec((blk, LR_BLOCK), lambda b, j: (row(b, j), 0))]

    state = (2, GLA_HEADS, GLA_DK, GLA_DV)
    in_specs = proj_specs(fwd) + proj_specs(bwd) + [
        pl.BlockSpec((2, LR_BLOCK, GLA_KW), lambda b, j: (0, 0, 0)),
        pl.BlockSpec((2, 1, GLA_KW), lambda b, j: (0, 0, 0)),
        pl.BlockSpec((2, GLA_CHUNK, GLA_CHUNK), lambda b, j: (0, 0, 0)),
        pl.BlockSpec((1,) + state, lambda b, j: (b, 0, 0, 0, 0))]
    args = [proj, proj, lr, proj, proj, lr, a_pad, bias, _gla_level_table(), s0]
    aliases = {}
    if states is not None:
        aliases = {len(args): 2}
        in_specs.append(pl.BlockSpec(memory_space=pl.ANY))
        args.append(states)
    return pl.pallas_call(
        _gla_kernel,
        grid=(batch, nb),
        in_specs=in_specs,
        out_specs=[pl.BlockSpec((blk, GLA_VW), lambda b, j: (fwd(b, j), 0)),
                   pl.BlockSpec((blk, GLA_VW), lambda b, j: (bwd(b, j), 0)),
                   pl.BlockSpec((1, 1) + state, lambda b, j: (b, layer, 0, 0, 0, 0))],
        out_shape=[jax.ShapeDtypeStruct((t, GLA_VW), BF16),
                   jax.ShapeDtypeStruct((t, GLA_VW), BF16),
                   jax.ShapeDtypeStruct((batch, depth) + state, F32)],
        scratch_shapes=[pltpu.VMEM((2, GLA_HEADS, GLA_DV, GLA_DK), F32)],
        input_output_aliases=aliases,
        compiler_params=pltpu.CompilerParams(dimension_semantics=("parallel", "arbitrary"),
                                             vmem_limit_bytes=VMEM_LIMIT),
        name="gla",
    )(*args)


HALO = 16


def _out_kernel(*refs, tm, seq_len, final):
    (x_ref, a_ref, ch_ref, cb_ref, cc_ref, cg_ref, chp_ref, ccp_ref, chn_ref, ccn_ref,
     gg_ref, of_ref, ob_ref, cw_ref, gnw_ref, gate_ref, w_ref) = refs[:17]
    pos = 17
    if final:
        fnw_ref = refs[pos]
        pos += 1
    o_ref = refs[pos]

    def prod(c_ref, h_ref, r):
        return c_ref[r:r + 1, :].astype(F32) * h_ref[r:r + 1, :].astype(F32)

    u = cc_ref[...].astype(F32) * ch_ref[...].astype(F32)
    row = lax.broadcasted_iota(jnp.int32, u.shape, 0)
    seq_pos = (pl.program_id(0) * tm + row) & (seq_len - 1)
    u_prev = jnp.where(row == 0, prod(ccp_ref, chp_ref, HALO - 1), pltpu.roll(u, 1, axis=0))
    u_prev = jnp.where(seq_pos == 0, 0.0, u_prev)
    u_next = jnp.where(row == tm - 1, prod(ccn_ref, chn_ref, 0), pltpu.roll(u, tm - 1, axis=0))
    u_next = jnp.where(seq_pos == seq_len - 1, 0.0, u_next)
    conv = u_prev * cw_ref[0:1, :] + u * cw_ref[1:2, :] + u_next * cw_ref[2:3, :]
    out_b = cb_ref[...].astype(F32) * conv * _silu(cg_ref[...].astype(F32))
    parts = [a_ref[...], out_b.astype(BF16)]

    o = of_ref[...].astype(F32) + ob_ref[...].astype(F32)
    gg = gg_ref[...].astype(F32)
    for h in range(GLA_HEADS):
        cols = slice(h * GLA_DV, (h + 1) * GLA_DV)
        parts.append((_rms(o[:, cols], gnw_ref[...]) * _silu(gg[:, cols])).astype(BF16))

    r = jnp.dot(jnp.concatenate(parts, axis=1), w_ref[0], preferred_element_type=F32)
    xn = x_ref[...] + gate_ref[0] * r
    if final:
        xn = _rms(xn, fnw_ref[...])
    o_ref[...] = xn


def _out_projection(x, att, proj, o_f, o_b, conv_w, gla_norm_w, mod3, w_out, layer, final_norm_w, mod_row,
                    seq_len):
    t, d = x.shape
    tm = 512
    cw = conv_w.shape[1]
    final = final_norm_w is not None
    per_halo = tm // HALO
    last_halo = t // HALO - 1

    def col(c, width):
        return lambda i: (i, c // width)

    prev = lambda c: (lambda i: (jnp.maximum(i * per_halo - 1, 0), c // cw))
    nxt = lambda c: (lambda i: (jnp.minimum((i + 1) * per_halo, last_halo), c // cw))
    in_specs = [pl.BlockSpec((tm, d), lambda i: (i, 0)),
                pl.BlockSpec((tm, ATT_WIDTH), lambda i: (i, 0)),
                pl.BlockSpec((tm, cw), col(COL_CH, cw)),
                pl.BlockSpec((tm, cw), col(COL_CB, cw)),
                pl.BlockSpec((tm, cw), col(COL_CC, cw)),
                pl.BlockSpec((tm, cw), col(COL_CG, cw)),
                pl.BlockSpec((HALO, cw), prev(COL_CH)),
                pl.BlockSpec((HALO, cw), prev(COL_CC)),
                pl.BlockSpec((HALO, cw), nxt(COL_CH)),
                pl.BlockSpec((HALO, cw), nxt(COL_CC)),
                pl.BlockSpec((tm, GLA_VW), col(COL_GG, GLA_VW)),
                pl.BlockSpec((tm, GLA_VW), lambda i: (i, 0)),
                pl.BlockSpec((tm, GLA_VW), lambda i: (i, 0)),
                pl.BlockSpec((3, cw), lambda i: (0, 0)),
                pl.BlockSpec((1, GLA_DV), lambda i: (0, 0)),
                pl.BlockSpec((1, 1, d), lambda i: (mod_row(i * tm), 0, 2)),
                pl.BlockSpec((1,) + w_out.shape[1:], lambda i: (layer, 0, 0))]
    args = [x, att, proj, proj, proj, proj, proj, proj, proj, proj, proj, o_f, o_b,
            conv_w, gla_norm_w.reshape(1, GLA_DV), mod3, w_out]
    if final:
        in_specs.append(pl.BlockSpec((1, d), lambda i: (0, 0)))
        args.append(final_norm_w.reshape(1, d))
    return pl.pallas_call(
        functools.partial(_out_kernel, tm=tm, seq_len=seq_len, final=final),
        grid=(t // tm,),
        in_specs=in_specs,
        out_specs=pl.BlockSpec((tm, d), lambda i: (i, 0)),
        out_shape=jax.ShapeDtypeStruct((t, d), F32),
        compiler_params=pltpu.CompilerParams(dimension_semantics=("parallel",),
                                             vmem_limit_bytes=VMEM_LIMIT),
        name="out_projection",
    )(*args)


def _rope_tables(seq_len):
    pos = jnp.arange(seq_len)
    row = (pos // GRID_W).astype(F32)
    col = (pos % GRID_W).astype(F32)
    n_freq = HEAD_DIM // 4
    inv = ROPE_THETA ** (-jnp.arange(n_freq, dtype=F32) / n_freq)
    ang_r = row[:, None] * inv[None, :]
    ang_c = col[:, None] * inv[None, :]
    cos = jnp.concatenate([jnp.cos(ang_r), jnp.cos(ang_r), jnp.cos(ang_c), jnp.cos(ang_c)], axis=-1)
    sin = jnp.concatenate([-jnp.sin(ang_r), jnp.sin(ang_r), -jnp.sin(ang_c), jnp.sin(ang_c)], axis=-1)
    return cos, sin


@jax.jit
def _forward(x_prompt, x_sample, cache_k, cache_v, state_gla, c, c_ctx, w_mod, b_mod, norm_w, w_in,
             q_norm_w, k_norm_w, conv_w, gla_a_up, gla_a_bias, gla_norm_w, w_out, final_norm_w):
    batch, seq, d = x_prompt.shape
    dec_batch, dec_seq, _ = x_sample.shape
    depth = w_in.shape[0]
    past = cache_k.shape[2]

    cond = jnp.zeros((8, d), F32).at[0].set(c_ctx).at[1:1 + dec_batch].set(c)
    mod3 = _modulation(cond, w_mod, b_mod).reshape(depth * 8, 1, 3 * d)
    rope_tables = _rope_tables(dec_seq)

    h = x_prompt.reshape(batch * seq, d)
    z = x_sample.reshape(dec_batch * dec_seq, d)
    zero_state = jnp.zeros((batch, 2, GLA_HEADS, GLA_DK, GLA_DV), F32)
    w_in_t = jnp.swapaxes(w_in, 1, 2).astype(BF16)
    w_lr_t = jnp.pad(w_in_t[:, COL_LR:, :], ((0, 0), (0, LR_BLOCK - (IN_WIDTH - COL_LR)), (0, 0)))
    w_out_b = w_out.astype(BF16)
    new_kv, new_state = None, None
    for l in range(depth):
        a_pad = jnp.zeros((2, LR_BLOCK, GLA_KW), F32)
        for zdir in range(2):
            a_pad = a_pad.at[zdir, zdir * GLA_RANK:(zdir + 1) * GLA_RANK].set(gla_a_up[l, zdir])
        a_pad = a_pad.astype(BF16)
        bias = gla_a_bias[l].reshape(2, 1, GLA_KW)
        final_w = final_norm_w if l == depth - 1 else None

        row_ctx = lambda tok, l=l: l * 8
        proj, lr = _in_projection(h, mod3, norm_w[l], w_in_t, w_lr_t, l, row_ctx)
        att, new_kv = _context_attention(proj, q_norm_w[l], k_norm_w[l], batch, seq, new_kv, l, depth)
        o_f, o_b, new_state = _gla(proj, lr, a_pad, bias, zero_state, batch, seq, new_state, l, depth)
        h = _out_projection(h, att, proj, o_f, o_b, conv_w[l], gla_norm_w[l], mod3, w_out_b, l, final_w,
                            row_ctx, seq)

        row_lat = lambda tok, l=l: l * 8 + 1 + tok // dec_seq
        proj, lr = _in_projection(z, mod3, norm_w[l], w_in_t, w_lr_t, l, row_lat)
        ctx_kv = (cache_k[:, l].reshape(dec_batch * past, KV_WIDTH),
                  cache_v[:, l].reshape(dec_batch * past, KV_WIDTH))
        k_all, vt_all, k_norm2 = _kv_prepare(proj, k_norm_w[l], rope_tables, ctx_kv, dec_batch, dec_seq)
        att = _attention(proj, k_all, vt_all, k_norm2, q_norm_w[l], rope_tables, dec_batch, dec_seq)
        o_f, o_b, _ = _gla(proj, lr, a_pad, bias, state_gla[:, l], dec_batch, dec_seq)
        z = _out_projection(z, att, proj, o_f, o_b, conv_w[l], gla_norm_w[l], mod3, w_out_b, l, final_w,
                            row_lat, dec_seq)

    kv_shape = (batch, depth, seq, ATT_KV_HEADS, HEAD_DIM)
    return (h.reshape(batch, seq, d), z.reshape(dec_batch, dec_seq, d),
            new_kv[0].reshape(kv_shape), new_kv[1].reshape(kv_shape), new_state)


def kernel(x_prompt, x_sample, cache_k, cache_v, state_gla, c, c_ctx, w_mod, b_mod, norm_w, w_in, q_norm_w,
           k_norm_w, conv_w, gla_a_up, gla_a_bias, gla_norm_w, w_out, final_norm_w):
    return _forward(x_prompt, x_sample, cache_k, cache_v, state_gla, c, c_ctx, w_mod, b_mod, norm_w, w_in,
                    q_norm_w, k_norm_w, conv_w, gla_a_up, gla_a_bias, gla_norm_w, w_out, final_norm_w)
```
